```python
import math
import jax, jax.numpy as jnp
from jax import lax
import numpy as np

D_MODEL = 1024
BATCH = 2
SEQ = 8192
DEPTH = 2

CTX_LEN = 256
GRID_W = 64
NORM_EPS = 1e-6
ROPE_BASE = 10000.0
Q_BLOCK = 128

GDN_HEADS = 4
GDN_DK = 128
GDN_DV = 128
GDN_CHUNK = 64
CONV_W = 5
GDN_WIDTH = GDN_HEADS * GDN_DV
GDN_QK = GDN_HEADS * GDN_DK
GDN_CONV_CH = 2 * GDN_QK + GDN_WIDTH
GDN_COLS = GDN_CONV_CH + GDN_WIDTH + 4 * GDN_HEADS

MLA_HEADS = 4
MLA_Q_RANK = 256
MLA_KV_RANK = 128
MLA_NOPE = 64
MLA_ROPE = 32
MLA_V = 64
MLA_WIDTH = MLA_HEADS * MLA_V
MLA_COLS = MLA_Q_RANK + MLA_KV_RANK + MLA_ROPE + MLA_WIDTH
MLA_SCALE = (MLA_NOPE + MLA_ROPE) ** -0.5

DIFF_HEADS = 4
DIFF_QK = 32
DIFF_V = 64
DIFF_WIDTH = DIFF_HEADS * DIFF_V
DIFF_QK_COLS = DIFF_HEADS * 2 * DIFF_QK
DIFF_COLS = 2 * DIFF_QK_COLS + 2 * DIFF_WIDTH
DIFF_SCALE = DIFF_QK ** -0.5

MIX_WIDTH = GDN_WIDTH + MLA_WIDTH + DIFF_WIDTH
IN_COLS = GDN_COLS + MLA_COLS + DIFF_COLS

kernel_name = 'hybrid_parallel_groups_dit_block'


def rms_norm(x, g):
    xf = x.astype(jnp.float32)
    y = xf * lax.rsqrt(jnp.mean(xf * xf, axis=-1, keepdims=True) + NORM_EPS)
    return (y * g.astype(jnp.float32)).astype(x.dtype)


def l2_normalize(x):
    return x * lax.rsqrt(jnp.sum(x * x, axis=-1, keepdims=True) + NORM_EPS)


def axial_rope_tables(n_rows, dim):
    row = jnp.repeat(jnp.arange(n_rows), GRID_W).astype(jnp.float32)
    col = jnp.tile(jnp.arange(GRID_W), n_rows).astype(jnp.float32)
    d_axis = dim // 2
    inv = 1.0 / (ROPE_BASE ** (jnp.arange(0, d_axis, 2, dtype=jnp.float32) / d_axis))
    ang = jnp.concatenate([row[:, None] * inv, col[:, None] * inv], axis=-1)
    return jnp.cos(ang), jnp.sin(ang)


def apply_rope(x, cos, sin):
    shape = (cos.shape[0],) + (1,) * (x.ndim - 3) + (cos.shape[1],)
    cs, sn = cos.reshape(shape), sin.reshape(shape)
    x1, x2 = jnp.split(x.astype(jnp.float32), 2, axis=-1)
    return jnp.concatenate([x1 * cs - x2 * sn, x1 * sn + x2 * cs], axis=-1).astype(x.dtype)


def sweep_query_blocks(attend, q):
    B, S = q.shape[:2]
    nb = S // Q_BLOCK
    qb = jnp.moveaxis(q.reshape((B, nb, Q_BLOCK) + q.shape[2:]), 1, 0)
    ob = lax.map(attend, qb)
    return jnp.moveaxis(ob, 0, 1).reshape((B, S) + ob.shape[3:])


def softmax_attention(q, k, v, scale):
    s = jnp.einsum('bqhd,bkhd->bhqk', q, k).astype(jnp.float32) * scale
    p = jax.nn.softmax(s, axis=-1).astype(v.dtype)
    return jnp.einsum('bhqk,bkhd->bqhd', p, v)


def differential_attention(q, k, v, lam):
    s = jnp.einsum('bqhmd,bkhmd->bhmqk', q, k).astype(jnp.float32) * DIFF_SCALE
    p = jax.nn.softmax(s, axis=-1)
    w = p[:, :, 0] - lam * p[:, :, 1]
    return jnp.einsum('bhqk,bkhd->bqhd', w.astype(v.dtype), v)


def centred_depthwise_conv(x, w):
    C = x.shape[-1]
    return lax.conv_general_dilated(
        x, w[:, None, :].astype(x.dtype), window_strides=(1,),
        padding=[(CONV_W // 2, CONV_W // 2)],
        dimension_numbers=('NWC', 'WIO', 'NWC'), feature_group_count=C)


def gated_delta_chunked(q, k, v, g, beta, s0):
    B, T, H, K = q.shape
    N = T // GDN_CHUNK

    def chunks(a):
        a = a.reshape((B, N, GDN_CHUNK, H) + a.shape[3:])
        return jnp.moveaxis(a, (1, 3), (0, 2))

    qc, kc, vc, gc, bc = (chunks(a) for a in (q * (K ** -0.5), k, v, g, beta))
    gam = jnp.cumsum(gc, axis=-1)
    idx = jnp.arange(GDN_CHUNK)
    incl = idx[:, None] >= idx[None, :]
    strict = idx[:, None] > idx[None, :]
    diff = gam[..., :, None] - gam[..., None, :]
    dec = jnp.where(incl, jnp.exp(jnp.where(incl, diff, 0.0)), 0.0)
    kb = kc * bc[..., None]
    a_mat = jnp.where(strict, jnp.einsum('nbhik,nbhjk->nbhij', kb, kc) * dec, 0.0)
    eye = jnp.eye(GDN_CHUNK, dtype=a_mat.dtype)
    t_mat = lax.linalg.triangular_solve(eye + a_mat, jnp.broadcast_to(eye, a_mat.shape),
                                        left_side=True, lower=True)
    eg = jnp.exp(gam)[..., None]
    u = t_mat @ (vc * bc[..., None])
    w = t_mat @ (kb * eg)
    qk = jnp.where(incl, jnp.einsum('nbhik,nbhjk->nbhij', qc, kc) * dec, 0.0)
    q_dec = qc * eg
    k_dec = kc * jnp.exp(gam[..., -1:] - gam)[..., None]
    g_last = jnp.exp(gam[..., -1])

    def step(s, xs):
        u_i, w_i, qk_i, qd_i, kd_i, gl_i = xs
        v_new = u_i - w_i @ s
        o = qd_i @ s + qk_i @ v_new
        s = s * gl_i[..., None, None] + jnp.swapaxes(kd_i, -1, -2) @ v_new
        return s, o

    s_fin, o = lax.scan(step, s0, (u, w, qk, q_dec, k_dec, g_last))
    o = jnp.moveaxis(o, (0, 2), (1, 3)).reshape(B, T, H, v.shape[-1])
    return o, s_fin


def gdn_prepare(p, conv_w, a_log, dt_bias):
    B, T = p.shape[:2]
    qkv, z, b_logit, a_logit = jnp.split(
        p, [GDN_CONV_CH, GDN_CONV_CH + GDN_WIDTH, GDN_CONV_CH + GDN_WIDTH + 2 * GDN_HEADS], axis=-1)
    qkv = jax.nn.silu(centred_depthwise_conv(qkv, conv_w)).astype(jnp.float32)
    q, k, v = jnp.split(qkv, [GDN_QK, 2 * GDN_QK], axis=-1)
    q = l2_normalize(q.reshape(B, T, GDN_HEADS, GDN_DK))
    k = l2_normalize(k.reshape(B, T, GDN_HEADS, GDN_DK))
    v = v.reshape(B, T, GDN_HEADS, GDN_DV)
    beta = jax.nn.sigmoid(b_logit.astype(jnp.float32)).reshape(B, T, 2, GDN_HEADS)
    g = -jnp.exp(a_log.astype(jnp.float32)) * jax.nn.softplus(
        a_logit.astype(jnp.float32).reshape(B, T, 2, GDN_HEADS) + dt_bias.astype(jnp.float32))
    return q, k, v, beta, g, z


def bidirectional_gated_delta(lat, ctx):
    q, k, v, beta, g = lat
    qc, kc, vc, betac, gc = ctx
    s0 = jnp.zeros((q.shape[0], GDN_HEADS, GDN_DK, GDN_DV), jnp.float32)
    flip = lambda a: jnp.flip(a, axis=1)
    oc_f, sc_f = gated_delta_chunked(qc, kc, vc, gc[:, :, 0], betac[:, :, 0], s0)
    ol_f, _ = gated_delta_chunked(q, k, v, g[:, :, 0], beta[:, :, 0], sc_f)
    oc_b, sc_b = gated_delta_chunked(flip(qc), flip(kc), flip(vc), flip(gc[:, :, 1]), flip(betac[:, :, 1]), s0)
    ol_b, _ = gated_delta_chunked(flip(q), flip(k), flip(v), flip(g[:, :, 1]), flip(beta[:, :, 1]), sc_b)
    return ol_f + flip(ol_b), oc_f + flip(oc_b)


def gdn_finish(o, z, out_gain, dtype):
    B, T = o.shape[:2]
    y = rms_norm(o, out_gain) * jax.nn.silu(z.astype(jnp.float32).reshape(B, T, GDN_HEADS, GDN_DV))
    return y.reshape(B, T, GDN_WIDTH).astype(dtype)


def mla_project(p, q_gain, w_uq, kv_gain, w_ukv, rope):
    B, T = p.shape[:2]
    c_q, c_kv, k_r, z = jnp.split(
        p, [MLA_Q_RANK, MLA_Q_RANK + MLA_KV_RANK, MLA_Q_RANK + MLA_KV_RANK + MLA_ROPE], axis=-1)
    q = (rms_norm(c_q, q_gain) @ w_uq).reshape(B, T, MLA_HEADS, MLA_NOPE + MLA_ROPE)
    kv = (rms_norm(c_kv, kv_gain) @ w_ukv).reshape(B, T, MLA_HEADS, MLA_NOPE + MLA_V)
    q_n, q_r = jnp.split(q, [MLA_NOPE], axis=-1)
    k_n, v = jnp.split(kv, [MLA_NOPE], axis=-1)
    k_r = k_r[:, :, None, :]
    if rope is not None:
        q_r = apply_rope(q_r, *rope)
        k_r = apply_rope(k_r, *rope)
    q = jnp.concatenate([q_n, q_r], axis=-1)
    k = jnp.concatenate([k_n, jnp.broadcast_to(k_r, (B, T, MLA_HEADS, MLA_ROPE)).astype(k_n.dtype)], axis=-1)
    return q, k, v, z


def diff_project(p, rope):
    B, T = p.shape[:2]
    q, k, v, z = jnp.split(p, [DIFF_QK_COLS, 2 * DIFF_QK_COLS, 2 * DIFF_QK_COLS + DIFF_WIDTH], axis=-1)
    q = q.reshape(B, T, DIFF_HEADS, 2, DIFF_QK)
    k = k.reshape(B, T, DIFF_HEADS, 2, DIFF_QK)
    if rope is not None:
        q = apply_rope(q, *rope)
        k = apply_rope(k, *rope)
    return q, k, v.reshape(B, T, DIFF_HEADS, DIFF_V), z


def diff_finish(o, z, sub_gain, lam_init, dtype):
    B, T = o.shape[:2]
    y = rms_norm(o, sub_gain) * (1.0 - lam_init) * jax.nn.silu(z.reshape(B, T, DIFF_HEADS, DIFF_V))
    return y.reshape(B, T, DIFF_WIDTH).astype(dtype)


def hybrid_layer(x, xc, c, c_ctx, ada_w, ada_b, pre_gain, post_gain, w_in, gdn_conv, gdn_a_log,
                 gdn_dt_bias, gdn_out_gain, mla_q_gain, mla_w_uq, mla_kv_gain, mla_w_ukv,
                 diff_lambda, diff_sub_gain, w_out, lam_init, rope_mla, rope_diff, need_ctx):
    B, S, _ = x.shape
    shift, scale, gate = jnp.split(jax.nn.silu(c) @ ada_w + ada_b, 3, axis=-1)
    shift_c, scale_c, gate_c = jnp.split(jax.nn.silu(c_ctx) @ ada_w + ada_b, 3, axis=-1)
    h = rms_norm(x, pre_gain) * (1.0 + scale[:, None, :]) + shift[:, None, :]
    hc = rms_norm(xc, pre_gain) * (1.0 + scale_c) + shift_c
    cuts = [GDN_COLS, GDN_COLS + MLA_COLS]
    p_a, p_b, p_c = jnp.split(h @ w_in, cuts, axis=-1)
    pc_a, pc_b, pc_c = jnp.split(hc @ w_in, cuts, axis=-1)

    qa, ka, va, ba, ga, za = gdn_prepare(p_a, gdn_conv, gdn_a_log, gdn_dt_bias)
    qca, kca, vca, bca, gca, zca = gdn_prepare(pc_a, gdn_conv, gdn_a_log, gdn_dt_bias)
    oa, oca = bidirectional_gated_delta((qa, ka, va, ba, ga), (qca, kca, vca, bca, gca))
    out_a = gdn_finish(oa, za, gdn_out_gain, x.dtype)

    qb, kb, vb, zb = mla_project(p_b, mla_q_gain, mla_w_uq, mla_kv_gain, mla_w_ukv, rope_mla)
    qcb, kcb, vcb, zcb = mla_project(pc_b, mla_q_gain, mla_w_uq, mla_kv_gain, mla_w_ukv, None)
    kb_all = jnp.concatenate([kb, kcb], axis=1)
    vb_all = jnp.concatenate([vb, vcb], axis=1)
    ob = sweep_query_blocks(lambda qq: softmax_attention(qq, kb_all, vb_all, MLA_SCALE), qb)
    out_b = (ob * jax.nn.silu(zb).reshape(B, S, MLA_HEADS, MLA_V)).reshape(B, S, MLA_WIDTH).astype(x.dtype)

    lq1, lk1, lq2, lk2 = diff_lambda.astype(jnp.float32)
    lam = jnp.exp(jnp.sum(lq1 * lk1)) - jnp.exp(jnp.sum(lq2 * lk2)) + lam_init
    qd, kd, vd, zd = diff_project(p_c, rope_diff)
    qcd, kcd, vcd, zcd = diff_project(pc_c, None)
    kd_all = jnp.concatenate([kd, kcd], axis=1)
    vd_all = jnp.concatenate([vd, vcd], axis=1)
    od = sweep_query_blocks(lambda qq: differential_attention(qq, kd_all, vd_all, lam), qd)
    out_c = diff_finish(od, zd, diff_sub_gain, lam_init, x.dtype)

    y = jnp.concatenate([out_a, out_b, out_c], axis=-1) @ w_out
    x = x + gate[:, None, :] * rms_norm(y, post_gain)

    if need_ctx:
        Bc, Tc = xc.shape[:2]
        oc_a = gdn_finish(oca, zca, gdn_out_gain, xc.dtype)
        oc_b = softmax_attention(qcb, kcb, vcb, MLA_SCALE)
        oc_b = (oc_b * jax.nn.silu(zcb).reshape(Bc, Tc, MLA_HEADS, MLA_V)).reshape(Bc, Tc, MLA_WIDTH).astype(xc.dtype)
        oc_c = diff_finish(differential_attention(qcd, kcd, vcd, lam), zcd, diff_sub_gain, lam_init, xc.dtype)
        yc = jnp.concatenate([oc_a, oc_b, oc_c], axis=-1) @ w_out
        xc = xc + gate_c * rms_norm(yc, post_gain)
    else:
        xc = None
    return x, xc


def setup_inputs(seed: int = 0) -> dict:
    key = jax.random.key(seed)
    ks = jax.random.split(key, 20)
    L = DEPTH

    def nrm(k, shape, s):
        return jax.random.normal(k, shape, jnp.float32) * s

    dt = jnp.exp(jax.random.uniform(ks[11], (L, 2, GDN_HEADS), jnp.float32,
                                    minval=math.log(1e-3), maxval=math.log(1e-1)))
    return {
        'x': nrm(ks[0], (BATCH, SEQ, D_MODEL), 1.0),
        'c': nrm(ks[1], (BATCH, D_MODEL), 1.0),
        'ctx': nrm(ks[2], (BATCH, CTX_LEN, D_MODEL), 1.0),
        'c_ctx': nrm(ks[3], (D_MODEL,), 1.0),
        'ada_w': nrm(ks[4], (L, D_MODEL, 3 * D_MODEL), 0.5 * D_MODEL ** -0.5),
        'ada_b': nrm(ks[5], (L, 3 * D_MODEL), 0.01),
        'pre_gain': 1.0 + nrm(ks[6], (L, D_MODEL), 0.02),
        'post_gain': 1.0 + nrm(ks[7], (L, D_MODEL), 0.02),
        'w_in': nrm(ks[8], (L, D_MODEL, IN_COLS), D_MODEL ** -0.5),
        'gdn_conv': nrm(ks[9], (L, CONV_W, GDN_CONV_CH), CONV_W ** -0.5),
        'gdn_a_log': jnp.log(jax.random.uniform(ks[10], (L, 2, GDN_HEADS), jnp.float32, minval=1.0, maxval=16.0)),
        'gdn_dt_bias': dt + jnp.log(-jnp.expm1(-dt)),
        'gdn_out_gain': 1.0 + nrm(ks[12], (L, GDN_DV), 0.02),
        'mla_q_gain': 1.0 + nrm(ks[13], (L, MLA_Q_RANK), 0.02),
        'mla_w_uq': nrm(ks[14], (L, MLA_Q_RANK, MLA_HEADS * (MLA_NOPE + MLA_ROPE)), MLA_Q_RANK ** -0.5),
        'mla_kv_gain': 1.0 + nrm(ks[15], (L, MLA_KV_RANK), 0.02),
        'mla_w_ukv': nrm(ks[16], (L, MLA_KV_RANK, MLA_HEADS * (MLA_NOPE + MLA_V)), MLA_KV_RANK ** -0.5),
        'diff_lambda': nrm(ks[17], (L, 4, DIFF_QK), 0.1),
        'diff_sub_gain': 1.0 + nrm(ks[18], (L, DIFF_V), 0.02),
        'w_out': nrm(ks[19], (L, MIX_WIDTH, D_MODEL), MIX_WIDTH ** -0.5),
    }


def reference(x, c, ctx, c_ctx, ada_w, ada_b, pre_gain, post_gain, w_in, gdn_conv, gdn_a_log,
              gdn_dt_bias, gdn_out_gain, mla_q_gain, mla_w_uq, mla_kv_gain, mla_w_ukv,
              diff_lambda, diff_sub_gain, w_out):
    ROWS = x.shape[1] // GRID_W
    rope_mla = axial_rope_tables(ROWS, MLA_ROPE)
    rope_diff = axial_rope_tables(ROWS, DIFF_QK)
    xc = ctx
    for l in range(DEPTH):
        lam_init = 0.8 - 0.6 * math.exp(-0.3 * l)
        x, xc = hybrid_layer(
            x, xc, c, c_ctx, ada_w[l], ada_b[l], pre_gain[l], post_gain[l], w_in[l], gdn_conv[l],
            gdn_a_log[l], gdn_dt_bias[l], gdn_out_gain[l], mla_q_gain[l], mla_w_uq[l], mla_kv_gain[l],
            mla_w_ukv[l], diff_lambda[l], diff_sub_gain[l], w_out[l], lam_init, rope_mla, rope_diff,
            l < DEPTH - 1)
    return x
```

```python
import functools
import math

import jax
import jax.numpy as jnp
from jax import lax
from jax.experimental import pallas as pl
from jax.experimental.pallas import tpu as pltpu

F32 = jnp.float32
BF16 = jnp.bfloat16

D_MODEL = 1024
GRID_W = 64
NORM_EPS = 1e-6
ROPE_BASE = 10000.0

HEADS = 4
GDN_DK = 128
GDN_DV = 128
GDN_CHUNK = 64
CONV_W = 5
GDN_WIDTH = HEADS * GDN_DV
GDN_QK = HEADS * GDN_DK
GDN_CONV_CH = 2 * GDN_QK + GDN_WIDTH

MLA_Q_RANK = 256
MLA_KV_RANK = 128
MLA_NOPE = 64
MLA_ROPE = 32
MLA_V = 64
MLA_WIDTH = HEADS * MLA_V
MLA_SCALE = (MLA_NOPE + MLA_ROPE) ** -0.5

DIFF_QK = 32
DIFF_V = 64
DIFF_WIDTH = HEADS * DIFF_V
DIFF_QK_COLS = HEADS * 2 * DIFF_QK
DIFF_SCALE = DIFF_QK ** -0.5

LOG2E = 1.4426950408889634

LANES = 128
SUBLANES = 8
VMEM_LIMIT = 48 * 1024 * 1024

ROW_TILE = 256
KV_BLOCK = 256
GDN_STEP = 2 * GDN_CHUNK
HEAD_PAD = LANES

_O_QKV = 0
_O_ZA = _O_QKV + GDN_CONV_CH
_O_BL = _O_ZA + GDN_WIDTH
_O_AL = _O_BL + 2 * HEADS
_O_CQ = _O_AL + 2 * HEADS
_O_CKV = _O_CQ + MLA_Q_RANK
_O_KR = _O_CKV + MLA_KV_RANK
_O_ZB = _O_KR + MLA_ROPE
_O_DQ = _O_ZB + MLA_WIDTH
_O_DK = _O_DQ + DIFF_QK_COLS
_O_DV = _O_DK + DIFF_QK_COLS
_O_DZ = _O_DV + DIFF_WIDTH

_PACK = {}
_off = 0
for _name, _w in (("qkv", GDN_CONV_CH), ("za", GDN_WIDTH), ("ba", LANES), ("cq", MLA_Q_RANK),
                  ("ckv", MLA_KV_RANK), ("kr", 2 * LANES), ("zb", MLA_WIDTH),
                  ("dq", 2 * HEADS * HEAD_PAD), ("dk", 2 * HEADS * HEAD_PAD),
                  ("dv", DIFF_WIDTH), ("dz", DIFF_WIDTH)):
    _PACK[_name] = (_off, _w)
    _off += _w
PACK_COLS = _off


def _silu(x):
    return x * jax.nn.sigmoid(x)


def _rms(x, gain):
    return x * lax.rsqrt(jnp.mean(x * x, axis=-1, keepdims=True) + NORM_EPS) * gain


def _dot(a, b):
    return jnp.dot(a, b, preferred_element_type=F32)


def _dot_nt(a, b):
    return lax.dot_general(a, b, (((1,), (1,)), ((), ())), preferred_element_type=F32)


def _mod_kernel(c_ref, w_ref, b_ref, o_ref):
    a = _silu(c_ref[...])
    o_ref[0] = jnp.dot(a, w_ref[0], preferred_element_type=F32,
                       precision=lax.Precision.HIGHEST) + b_ref[0]


def _modulation(cc, ada_w, ada_b):
    depth = ada_w.shape[0]
    ncol = ada_w.shape[2]
    blk = D_MODEL
    return pl.pallas_call(
        _mod_kernel,
        grid=(depth, ncol // blk),
        in_specs=[
            pl.BlockSpec((SUBLANES, D_MODEL), lambda l, j: (0, 0)),
            pl.BlockSpec((1, D_MODEL, blk), lambda l, j: (l, 0, j)),
            pl.BlockSpec((1, 1, blk), lambda l, j: (l, 0, j)),
        ],
        out_specs=pl.BlockSpec((1, SUBLANES, blk), lambda l, j: (l, 0, j)),
        out_shape=jax.ShapeDtypeStruct((depth, SUBLANES, ncol), F32),
        compiler_params=pltpu.CompilerParams(
            dimension_semantics=("arbitrary", "arbitrary"), vmem_limit_bytes=VMEM_LIMIT),
        name="adaln_mod",
    )(cc, ada_w, ada_b.reshape(depth, 1, ncol))


def _in_kernel(x_ref, mod_ref, pg_ref, w_ref, alog_ref, dtb_ref, qg_ref, wuq_ref, kvg_ref, wukv_ref,
               cm_ref, sm_ref, cd_ref, sd_ref,
               qkv_ref, za_ref, gb_ref, qm_ref, km_ref, vtm_ref, zb_ref,
               qd_ref, kd_ref, vtd_ref, zd_ref):
    x = x_ref[0]
    mod = mod_ref[0, 0]
    shift = mod[0:1]
    scale = mod[1:2]
    h = (_rms(x, pg_ref[...]) * (1.0 + scale) + shift).astype(BF16)

    def proj(name):
        off, width = _PACK[name]
        return _dot(h, w_ref[:, off:off + width])

    qkv_ref[0] = proj("qkv")
    za_ref[0] = proj("za")
    ba = proj("ba")
    lane = lax.broadcasted_iota(jnp.int32, ba.shape, 1)
    beta = jax.nn.sigmoid(ba)
    zz = ba + dtb_ref[...]
    softplus = jnp.maximum(zz, 0.0) + jnp.log(1.0 + jnp.exp(-jnp.abs(zz)))
    gdec = -jnp.exp(alog_ref[...]) * softplus
    gb_ref[0] = jnp.where(lane < 2 * HEADS, beta, jnp.where(lane < 4 * HEADS, gdec, 0.0))

    cm = cm_ref[...]
    sm = sm_ref[...]
    nq = _rms(proj("cq"), qg_ref[...]).astype(BF16)
    qq = _dot(nq, wuq_ref[...])
    nkv = _rms(proj("ckv"), kvg_ref[...]).astype(BF16)
    kvv = _dot(nkv, wukv_ref[...])
    kr = proj("kr")
    krr = kr[:, :LANES] * cm + kr[:, LANES:] * sm
    half = HEADS * HEAD_PAD
    for hh in range(HEADS):
        sl = slice(hh * HEAD_PAD, (hh + 1) * HEAD_PAD)
        sw = slice(half + hh * HEAD_PAD, half + (hh + 1) * HEAD_PAD)
        qm_ref[0, :, sl] = ((qq[:, sl] * cm + qq[:, sw] * sm) * (MLA_SCALE * LOG2E)).astype(BF16)
        km_ref[0, :, sl] = (kvv[:, sl] + krr).astype(BF16)
    vtm_ref[0, 0] = kvv[:, half:half + MLA_WIDTH].T.astype(BF16)
    zb_ref[0] = proj("zb")

    cd = cd_ref[...]
    sd = sd_ref[...]
    dq = proj("dq")
    dk = proj("dk")
    for hh in range(HEADS):
        sl = slice(hh * HEAD_PAD, (hh + 1) * HEAD_PAD)
        sw = slice(half + hh * HEAD_PAD, half + (hh + 1) * HEAD_PAD)
        qd_ref[0, :, sl] = ((dq[:, sl] * cd + dq[:, sw] * sd) * (DIFF_SCALE * LOG2E)).astype(BF16)
        kd_ref[0, :, sl] = (dk[:, sl] * cd + dk[:, sw] * sd).astype(BF16)
    vtd_ref[0, 0] = proj("dv").T.astype(BF16)
    zd_ref[0] = proj("dz")


def _in_proj(xa, mod_l, pre_gain, w_pack, alog_row, dtb_row, q_gain, wuq, kv_gain, wukv, tabs, n_ctx_tiles):
    bsz, t_all, _ = xa.shape
    tm = ROW_TILE
    nt = t_all // tm
    full = lambda shape: pl.BlockSpec(shape, lambda b, i: (0,) * len(shape))
    row = lambda width: pl.BlockSpec((1, tm, width), lambda b, i: (b, i, 0))
    tab = pl.BlockSpec((tm, LANES), lambda b, i: (i, 0))
    vt = pl.BlockSpec((1, 1, MLA_WIDTH, tm), lambda b, i: (b, i, 0, 0))
    act = lambda width, dt: jax.ShapeDtypeStruct((bsz, t_all, width), dt)
    vts = jax.ShapeDtypeStruct((bsz, nt, MLA_WIDTH, tm), BF16)
    return pl.pallas_call(
        _in_kernel,
        grid=(bsz, nt),
        in_specs=[
            row(D_MODEL),
            pl.BlockSpec((1, 1, 3, D_MODEL), lambda b, i: (b, jnp.where(i < n_ctx_tiles, 0, 1), 0, 0)),
            full((1, D_MODEL)),
            full((D_MODEL, PACK_COLS)),
            full((1, LANES)), full((1, LANES)),
            full((1, MLA_Q_RANK)), full((MLA_Q_RANK, 2 * HEADS * HEAD_PAD)),
            full((1, MLA_KV_RANK)), full((MLA_KV_RANK, HEADS * HEAD_PAD + MLA_WIDTH)),
            tab, tab, tab, tab,
        ],
        out_specs=[row(GDN_CONV_CH), row(GDN_WIDTH), row(LANES),
                   row(HEADS * HEAD_PAD), row(HEADS * HEAD_PAD), vt, row(MLA_WIDTH),
                   row(HEADS * HEAD_PAD), row(HEADS * HEAD_PAD), vt, row(DIFF_WIDTH)],
        out_shape=[act(GDN_CONV_CH, F32), act(GDN_WIDTH, F32), act(LANES, F32),
                   act(HEADS * HEAD_PAD, BF16), act(HEADS * HEAD_PAD, BF16), vts, act(MLA_WIDTH, F32),
                   act(HEADS * HEAD_PAD, BF16), act(HEADS * HEAD_PAD, BF16), vts, act(DIFF_WIDTH, F32)],
        compiler_params=pltpu.CompilerParams(
            dimension_semantics=("parallel", "arbitrary"), vmem_limit_bytes=VMEM_LIMIT),
        name="in_proj",
    )(xa, mod_l, pre_gain, w_pack, alog_row, dtb_row, q_gain, wuq, kv_gain, wukv, *tabs)


def _gprep_kernel(n_ctx_tiles, cur_ref, prev_ref, next_ref, w_ref, o_ref, ext_ref):
    i = pl.program_id(1)
    nt = pl.num_programs(1)
    tm = cur_ref.shape[1]
    halo = CONV_W // 2
    has_prev = jnp.logical_and(i != 0, i != n_ctx_tiles)
    has_next = jnp.logical_and(i != n_ctx_tiles - 1, i != nt - 1)
    ext_ref[0:SUBLANES] = jnp.where(has_prev, prev_ref[0], 0.0)
    ext_ref[SUBLANES:SUBLANES + tm] = cur_ref[0]
    ext_ref[SUBLANES + tm:2 * SUBLANES + tm] = jnp.where(has_next, next_ref[0], 0.0)
    acc = None
    for j in range(CONV_W):
        term = ext_ref[pl.ds(SUBLANES - halo + j, tm), :] * w_ref[j:j + 1, :]
        acc = term if acc is None else acc + term
    y = _silu(acc)
    for hh in range(2 * HEADS):
        sl = slice(hh * GDN_DK, (hh + 1) * GDN_DK)
        v = y[:, sl]
        v = v * lax.rsqrt(jnp.sum(v * v, axis=-1, keepdims=True) + NORM_EPS)
        if hh < HEADS:
            v = v * (GDN_DK ** -0.5)
        o_ref[0, :, sl] = v
    o_ref[0, :, 2 * GDN_QK:] = y[:, 2 * GDN_QK:]


def _gdn_prepare(p_qkv, conv_w, n_ctx_tiles):
    bsz, t_all, ch = p_qkv.shape
    tm = ROW_TILE
    nt = t_all // tm
    per = tm // SUBLANES
    last = t_all // SUBLANES - 1
    return pl.pallas_call(
        functools.partial(_gprep_kernel, n_ctx_tiles),
        grid=(bsz, nt),
        in_specs=[
            pl.BlockSpec((1, tm, ch), lambda b, i: (b, i, 0)),
            pl.BlockSpec((1, SUBLANES, ch), lambda b, i: (b, jnp.maximum(i * per - 1, 0), 0)),
            pl.BlockSpec((1, SUBLANES, ch), lambda b, i: (b, jnp.minimum((i + 1) * per, last), 0)),
            pl.BlockSpec((CONV_W, ch), lambda b, i: (0, 0)),
        ],
        out_specs=pl.BlockSpec((1, tm, ch), lambda b, i: (b, i, 0)),
        out_shape=jax.ShapeDtypeStruct((bsz, t_all, ch), F32),
        scratch_shapes=[pltpu.VMEM((tm + 2 * SUBLANES, ch), F32)],
        compiler_params=pltpu.CompilerParams(
            dimension_semantics=("parallel", "arbitrary"), vmem_limit_bytes=VMEM_LIMIT),
        name="gdn_prep",
    )(p_qkv, p_qkv, p_qkv, conv_w)


def _mm(a, b):
    return _dot(a.astype(BF16), b.astype(BF16))


def _unit_lower_inverse(a_mat, eye, blk16):
    d = jnp.where(blk16, a_mat, 0.0)
    n = a_mat - d
    dinv = eye - d
    dk = d
    for _ in range(3):
        dk = _mm(dk, dk)
        dinv = dinv + _mm(dinv, dk)
    m = _mm(dinv, n)
    m2 = _mm(m, m)
    imm = eye - m
    minv = imm + _mm(imm, m2)
    return _mm(minv, dinv)


def _gscan_kernel(qf_ref, gf_ref, qb_ref, gb_ref, of_ref, ob_ref, s_ref):
    step = pl.program_id(1)

    @pl.when(step == 0)
    def _():
        s_ref[...] = jnp.zeros_like(s_ref)

    n = GDN_STEP
    c = GDN_CHUNK
    r = lax.broadcasted_iota(jnp.int32, (n, n), 0)
    col = lax.broadcasted_iota(jnp.int32, (n, n), 1)
    same = (r >= c) == (col >= c)
    eye = jnp.where(r == col, 1.0, 0.0).astype(F32)
    blk16 = (r // 16) == (col // 16)
    first_rows = r < c
    first_cols = col < c

    for d, (q_ref, g_ref, o_ref) in enumerate(((qf_ref, gf_ref, of_ref), (qb_ref, gb_ref, ob_ref))):
        if d == 0:
            incl = jnp.logical_and(same, r >= col)
            strict = jnp.logical_and(same, r > col)
            last_a, last_b = c - 1, n - 1
            order = (0, 1)
        else:
            incl = jnp.logical_and(same, r <= col)
            strict = jnp.logical_and(same, r < col)
            last_a, last_b = 0, c
            order = (1, 0)
        gbv = g_ref[0]
        gam = jnp.dot(jnp.where(incl, 1.0, 0.0).astype(F32), gbv, preferred_element_type=F32,
                      precision=lax.Precision.HIGHEST)
        gam_t = gam.T
        glast = jnp.where(first_rows, gam[last_a:last_a + 1, :], gam[last_b:last_b + 1, :])
        eg_all = jnp.exp(gam)
        kdf_all = jnp.exp(glast - gam)
        egl_all = jnp.exp(glast)
        for hh in range(HEADS):
            cb = d * HEADS + hh
            cg = 2 * HEADS + d * HEADS + hh
            chain = d * HEADS + hh
            q = q_ref[0, :, hh * GDN_DK:(hh + 1) * GDN_DK]
            k = q_ref[0, :, GDN_QK + hh * GDN_DK:GDN_QK + (hh + 1) * GDN_DK]
            v = q_ref[0, :, 2 * GDN_QK + hh * GDN_DV:2 * GDN_QK + (hh + 1) * GDN_DV]
            beta = gbv[:, cb:cb + 1]
            diff = gam[:, cg:cg + 1] - gam_t[cg:cg + 1, :]
            dec = jnp.where(incl, jnp.exp(jnp.where(incl, diff, 0.0)), 0.0)
            eg = eg_all[:, cg:cg + 1]
            kb = k * beta
            kk_qk = _dot_nt(jnp.concatenate([kb, q], axis=0).astype(BF16), k.astype(BF16))
            a_mat = jnp.where(strict, kk_qk[:n] * dec, 0.0)
            qkm = jnp.where(incl, kk_qk[n:] * dec, 0.0)
            t_mat = _unit_lower_inverse(a_mat, eye, blk16)
            uw = _mm(t_mat, jnp.concatenate([v * beta, kb * eg], axis=1))
            u = uw[:, :GDN_DV]
            w = uw[:, GDN_DV:]
            qd = q * eg
            kd_t = (k * kdf_all[:, cg:cg + 1]).T
            s = s_ref[chain]
            vn = [None, None]
            qs = [None, None]
            for ci in order:
                rows = slice(ci * c, (ci + 1) * c)
                wsqs = _mm(jnp.concatenate([w[rows], qd[rows]], axis=0), s)
                vn[ci] = u[rows] - wsqs[:c]
                qs[ci] = wsqs[c:]
                zero = jnp.zeros_like(vn[ci])
                vfull = jnp.concatenate([vn[ci], zero] if ci == 0 else [zero, vn[ci]], axis=0)
                kd_c = jnp.where(first_cols if ci == 0 else jnp.logical_not(first_cols), kd_t, 0.0)
                lrow = last_a if ci == 0 else last_b
                s = s * egl_all[lrow:lrow + 1, cg:cg + 1] + _mm(kd_c, vfull)
            s_ref[chain] = s
            o = jnp.concatenate(qs, axis=0) + _mm(qkm, jnp.concatenate(vn, axis=0))
            o_ref[0, :, hh * GDN_DV:(hh + 1) * GDN_DV] = o


def _gdn_scan(qkv_n, gb, n_ctx_steps):
    bsz, t_all, ch = qkv_n.shape
    n = GDN_STEP
    nsteps = t_all // n

    def bwd(i):
        return jnp.where(i < n_ctx_steps, n_ctx_steps - 1 - i, nsteps - 1 - (i - n_ctx_steps))

    out = jax.ShapeDtypeStruct((bsz, t_all, GDN_WIDTH), F32)
    return pl.pallas_call(
        _gscan_kernel,
        grid=(bsz, nsteps),
        in_specs=[
            pl.BlockSpec((1, n, ch), lambda b, i: (b, i, 0)),
            pl.BlockSpec((1, n, LANES), lambda b, i: (b, i, 0)),
            pl.BlockSpec((1, n, ch), lambda b, i: (b, bwd(i), 0)),
            pl.BlockSpec((1, n, LANES), lambda b, i: (b, bwd(i), 0)),
        ],
        out_specs=[
            pl.BlockSpec((1, n, GDN_WIDTH), lambda b, i: (b, i, 0)),
            pl.BlockSpec((1, n, GDN_WIDTH), lambda b, i: (b, bwd(i), 0)),
        ],
        out_shape=[out, out],
        scratch_shapes=[pltpu.VMEM((2 * HEADS, GDN_DK, GDN_DV), F32)],
        compiler_params=pltpu.CompilerParams(
            dimension_semantics=("parallel", "arbitrary"), vmem_limit_bytes=VMEM_LIMIT),
        name="gdn_scan",
    )(qkv_n, gb, qkv_n, gb)


def _attn_kernel(n_maps, n_ctx_blocks, lam_init, q_ref, k_ref, vt_ref, z_ref, gain_ref, lam_ref,
                 o_ref, m_ref, l_ref, acc_ref):
    i = pl.program_id(1)
    tq = q_ref.shape[1]
    nblk = vt_ref.shape[1]
    nhm = HEADS * n_maps
    dv = vt_ref.shape[2] // HEADS

    m_ref[...] = jnp.full_like(m_ref, -1e30)
    l_ref[...] = jnp.zeros_like(l_ref)
    acc_ref[...] = jnp.zeros_like(acc_ref)

    lane = lax.broadcasted_iota(jnp.int32, (tq, HEAD_PAD), 1)
    qs = []
    for hh in range(HEADS):
        qh = q_ref[0, :, hh * HEAD_PAD:(hh + 1) * HEAD_PAD]
        for mm in range(n_maps):
            if n_maps == 1:
                qs.append(qh)
            else:
                keep = jnp.logical_and(lane >= mm * DIFF_QK, lane < (mm + 1) * DIFF_QK)
                qs.append(jnp.where(keep, qh, jnp.zeros_like(qh)))

    def attend(blocks):
        for hh in range(HEADS):
            ks = [k_ref[0, pl.ds(pl.multiple_of(blk * KV_BLOCK, KV_BLOCK), KV_BLOCK),
                        hh * HEAD_PAD:(hh + 1) * HEAD_PAD] for blk in blocks]
            vts = [vt_ref[0, blk, hh * dv:(hh + 1) * dv, :] for blk in blocks]
            for mm in range(n_maps):
                idx = hh * n_maps + mm
                st = [_dot_nt(kk, qs[idx]) for kk in ks]
                m_prev = m_ref[idx]
                m_cur = functools.reduce(jnp.maximum, [jnp.max(s, axis=0, keepdims=True) for s in st])
                m_new = jnp.maximum(m_prev, m_cur)
                alpha = jnp.exp2(m_prev - m_new)
                ps = [jnp.exp2(s - m_new) for s in st]
                l_ref[idx] = alpha * l_ref[idx] + functools.reduce(
                    jnp.add, [jnp.sum(p, axis=0, keepdims=True) for p in ps])
                pv = functools.reduce(jnp.add, [_dot(vv, p.astype(BF16)) for vv, p in zip(vts, ps)])
                acc_ref[idx] = alpha * acc_ref[idx] + pv
                m_ref[idx] = m_new

    for blk in range(n_ctx_blocks):
        attend([blk])

    n_pairs = jnp.where(i < n_ctx_blocks, 0, (nblk - n_ctx_blocks) // 2)

    def body(j, carry):
        b0 = n_ctx_blocks + 2 * j
        attend([b0, b0 + 1])
        return carry

    lax.fori_loop(0, n_pairs, body, 0)

    outs = []
    for hh in range(HEADS):
        if n_maps == 1:
            o_t = acc_ref[hh] / l_ref[hh]
        else:
            lam = lam_ref[...]
            lq = jnp.sum(lam[0:1] * lam[1:2], axis=-1, keepdims=True)
            lk = jnp.sum(lam[2:3] * lam[3:4], axis=-1, keepdims=True)
            lam_full = jnp.exp(lq) - jnp.exp(lk) + lam_init
            o_t = acc_ref[2 * hh] / l_ref[2 * hh] - lam_full * (acc_ref[2 * hh + 1] / l_ref[2 * hh + 1])
            ms = jnp.mean(o_t * o_t, axis=0, keepdims=True)
            o_t = o_t * lax.rsqrt(ms + NORM_EPS) * gain_ref[...] * (1.0 - lam_init)
        outs.append(o_t)
    o = jnp.concatenate(outs, axis=0).T
    o_ref[0] = o * _silu(z_ref[0])


def _attention(n_maps, lam_init, q, k, vt, z, gain_col, lam, n_ctx_blocks):
    bsz, t_all, width = q.shape
    tq = ROW_TILE
    nq = t_all // tq
    nblk = vt.shape[1]
    hv = vt.shape[2]
    nhm = HEADS * n_maps
    return pl.pallas_call(
        functools.partial(_attn_kernel, n_maps, n_ctx_blocks, lam_init),
        grid=(bsz, nq),
        in_specs=[
            pl.BlockSpec((1, tq, width), lambda b, i: (b, i, 0)),
            pl.BlockSpec((1, t_all, width), lambda b, i: (b, 0, 0)),
            pl.BlockSpec((1, nblk, hv, KV_BLOCK), lambda b, i: (b, 0, 0, 0)),
            pl.BlockSpec((1, tq, hv), lambda b, i: (b, i, 0)),
            pl.BlockSpec(gain_col.shape, lambda b, i: (0, 0)),
            pl.BlockSpec(lam.shape, lambda b, i: (0, 0)),
        ],
        out_specs=pl.BlockSpec((1, tq, hv), lambda b, i: (b, i, 0)),
        out_shape=jax.ShapeDtypeStruct((bsz, t_all, hv), F32),
        scratch_shapes=[pltpu.VMEM((nhm, 1, tq), F32), pltpu.VMEM((nhm, 1, tq), F32),
                        pltpu.VMEM((nhm, hv // HEADS, tq), F32)],
        compiler_params=pltpu.CompilerParams(
            dimension_semantics=("parallel", "arbitrary"), vmem_limit_bytes=VMEM_LIMIT),
        name="mla_attn" if n_maps == 1 else "diff_attn",
    )(q, k, vt, z, gain_col, lam)


def _out_kernel(x_ref, mod_ref, of_ref, ob_ref, za_ref, ag_ref, b_ref, c_ref, w_ref, pg_ref, o_ref):
    o_a = of_ref[0] + ob_ref[0]
    za = za_ref[0]
    parts = []
    for hh in range(HEADS):
        sl = slice(hh * GDN_DV, (hh + 1) * GDN_DV)
        parts.append(_rms(o_a[:, sl], ag_ref[...]) * _silu(za[:, sl]))
    parts.append(b_ref[0])
    parts.append(c_ref[0])
    cat = jnp.concatenate(parts, axis=1).astype(BF16)
    y = _dot(cat, w_ref[...])
    gate = mod_ref[0, 0][2:3]
    o_ref[0] = x_ref[0] + gate * _rms(y, pg_ref[...])


def _out_proj(xa, mod_l, o_f, o_b, z_a, a_gain, out_b, out_c, w_out, post_gain, n_ctx_tiles):
    bsz, t_all, _ = xa.shape
    tm = ROW_TILE
    nt = t_all // tm
    row = lambda width: pl.BlockSpec((1, tm, width), lambda b, i: (b, i, 0))
    full = lambda shape: pl.BlockSpec(shape, lambda b, i: (0,) * len(shape))
    return pl.pallas_call(
        _out_kernel,
        grid=(bsz, nt),
        in_specs=[
            row(D_MODEL),
            pl.BlockSpec((1, 1, 3, D_MODEL), lambda b, i: (b, jnp.where(i < n_ctx_tiles, 0, 1), 0, 0)),
            row(GDN_WIDTH), row(GDN_WIDTH), row(GDN_WIDTH), full((1, GDN_DV)),
            row(MLA_WIDTH), row(DIFF_WIDTH), full((D_MODEL, D_MODEL)), full((1, D_MODEL)),
        ],
        out_specs=row(D_MODEL),
        out_shape=jax.ShapeDtypeStruct(xa.shape, F32),
        compiler_params=pltpu.CompilerParams(
            dimension_semantics=("parallel", "arbitrary"), vmem_limit_bytes=VMEM_LIMIT),
        name="out_proj",
    )(xa, mod_l, o_f, o_b, z_a, a_gain, out_b, out_c, w_out, post_gain)


def _swap_halves(w):
    half = w.shape[-1] // 2
    return jnp.concatenate([w[..., half:], w[..., :half]], axis=-1)


def _pack_w_in(w):
    d = w.shape[0]
    z = lambda n: jnp.zeros((d, n), w.dtype)
    kr = w[:, _O_KR:_O_KR + MLA_ROPE]
    pad_r = HEAD_PAD - MLA_NOPE - MLA_ROPE
    kr_group = jnp.concatenate([z(MLA_NOPE), kr, z(pad_r), z(MLA_NOPE), _swap_halves(kr), z(pad_r)], axis=1)

    def diff_group(off):
        plain, swapped = [], []
        for hh in range(HEADS):
            m0 = w[:, off + hh * 2 * DIFF_QK:off + hh * 2 * DIFF_QK + DIFF_QK]
            m1 = w[:, off + hh * 2 * DIFF_QK + DIFF_QK:off + (hh + 1) * 2 * DIFF_QK]
            plain += [m0, m1, z(HEAD_PAD - 2 * DIFF_QK)]
            swapped += [_swap_halves(m0), _swap_halves(m1), z(HEAD_PAD - 2 * DIFF_QK)]
        return jnp.concatenate(plain + swapped, axis=1)

    groups = [
        w[:, _O_QKV:_O_QKV + GDN_CONV_CH],
        w[:, _O_ZA:_O_ZA + GDN_WIDTH],
        jnp.concatenate([w[:, _O_BL:_O_BL + 4 * HEADS], z(LANES - 4 * HEADS)], axis=1),
        w[:, _O_CQ:_O_CQ + MLA_Q_RANK],
        w[:, _O_CKV:_O_CKV + MLA_KV_RANK],
        kr_group,
        w[:, _O_ZB:_O_ZB + MLA_WIDTH],
        diff_group(_O_DQ),
        diff_group(_O_DK),
        w[:, _O_DV:_O_DV + DIFF_WIDTH],
        w[:, _O_DZ:_O_DZ + DIFF_WIDTH],
    ]
    return jnp.concatenate(groups, axis=1).astype(BF16)


def _pack_w_uq(w):
    d = w.shape[0]
    z = lambda n: jnp.zeros((d, n), w.dtype)
    per = MLA_NOPE + MLA_ROPE
    pad_r = HEAD_PAD - per
    plain, swapped = [], []
    for hh in range(HEADS):
        nope = w[:, hh * per:hh * per + MLA_NOPE]
        rope = w[:, hh * per + MLA_NOPE:(hh + 1) * per]
        plain += [nope, rope, z(pad_r)]
        swapped += [z(MLA_NOPE), _swap_halves(rope), z(pad_r)]
    return jnp.concatenate(plain + swapped, axis=1).astype(BF16)


def _pack_w_ukv(w):
    d = w.shape[0]
    z = lambda n: jnp.zeros((d, n), w.dtype)
    per = MLA_NOPE + MLA_V
    keys, vals = [], []
    for hh in range(HEADS):
        keys += [w[:, hh * per:hh * per + MLA_NOPE], z(HEAD_PAD - MLA_NOPE)]
        vals.append(w[:, hh * per + MLA_NOPE:(hh + 1) * per])
    return jnp.concatenate(keys + vals, axis=1).astype(BF16)


def _rope_tables(seq, ctx):
    n_rows = seq // GRID_W
    row = jnp.repeat(jnp.arange(n_rows), GRID_W).astype(F32)
    colp = jnp.tile(jnp.arange(GRID_W), n_rows).astype(F32)
    d_axis = MLA_ROPE // 2
    inv = 1.0 / (ROPE_BASE ** (jnp.arange(0, d_axis, 2, dtype=F32) / d_axis))
    ang = jnp.concatenate([row[:, None] * inv, colp[:, None] * inv], axis=-1)
    cos = jnp.concatenate([jnp.ones((ctx, ang.shape[1]), F32), jnp.cos(ang)], axis=0)
    sin = jnp.concatenate([jnp.zeros((ctx, ang.shape[1]), F32), jnp.sin(ang)], axis=0)
    t_all = seq + ctx
    ones = lambda n: jnp.ones((t_all, n), F32)
    zeros = lambda n: jnp.zeros((t_all, n), F32)
    pad_r = HEAD_PAD - MLA_NOPE - MLA_ROPE
    cos_m = jnp.concatenate([ones(MLA_NOPE), cos, cos, ones(pad_r)], axis=1)
    sin_m = jnp.concatenate([zeros(MLA_NOPE), -sin, sin, zeros(pad_r)], axis=1)
    pad_d = HEAD_PAD - 2 * DIFF_QK
    cos_d = jnp.concatenate([cos, cos, cos, cos, ones(pad_d)], axis=1)
    sin_d = jnp.concatenate([-sin, sin, -sin, sin, zeros(pad_d)], axis=1)
    return cos_m, sin_m, cos_d, sin_d


def _lane_row(p):
    flat = p.reshape(1, 2 * HEADS).astype(F32)
    return jnp.concatenate([jnp.zeros((1, 2 * HEADS), F32), flat,
                            jnp.zeros((1, LANES - 4 * HEADS), F32)], axis=1)


def kernel(x, c, ctx, c_ctx, ada_w, ada_b, pre_gain, post_gain, w_in, gdn_conv, gdn_a_log, gdn_dt_bias,
           gdn_out_gain, mla_q_gain, mla_w_uq, mla_kv_gain, mla_w_ukv, diff_lambda, diff_sub_gain, w_out):
    bsz, seq, _ = x.shape
    n_ctx = ctx.shape[1]
    depth = ada_w.shape[0]
    assert n_ctx % ROW_TILE == 0 and seq % (2 * KV_BLOCK) == 0 and seq % GRID_W == 0
    assert bsz + 1 <= SUBLANES
    n_ctx_tiles = n_ctx // ROW_TILE

    xa = jnp.concatenate([ctx, x], axis=1)
    cc = jnp.concatenate([c, c_ctx[None, :], jnp.zeros((SUBLANES - bsz - 1, D_MODEL), F32)], axis=0)
    mod = _modulation(cc, ada_w, ada_b)
    tabs = _rope_tables(seq, n_ctx)

    for l in range(depth):
        lam_init = 0.8 - 0.6 * math.exp(-0.3 * l)
        mod_rows = mod[l].reshape(SUBLANES, 3, D_MODEL)
        mod_l = jnp.stack([jnp.broadcast_to(mod_rows[bsz], (bsz, 3, D_MODEL)), mod_rows[:bsz]], axis=1)
        (p_qkv, z_a, gb, q_m, k_m, vt_m, z_b, q_d, k_d, vt_d, z_d) = _in_proj(
            xa, mod_l, pre_gain[l][None, :], _pack_w_in(w_in[l]),
            _lane_row(gdn_a_log[l]), _lane_row(gdn_dt_bias[l]),
            mla_q_gain[l][None, :], _pack_w_uq(mla_w_uq[l]),
            mla_kv_gain[l][None, :], _pack_w_ukv(mla_w_ukv[l]), tabs, n_ctx_tiles)
        qkv_n = _gdn_prepare(p_qkv, gdn_conv[l], n_ctx_tiles)
        o_f, o_b = _gdn_scan(qkv_n, gb, n_ctx // GDN_STEP)
        gain_col = diff_sub_gain[l][:, None]
        out_b = _attention(1, lam_init, q_m, k_m, vt_m, z_b, gain_col, diff_lambda[l], n_ctx // KV_BLOCK)
        out_c = _attention(2, lam_init, q_d, k_d, vt_d, z_d, gain_col, diff_lambda[l], n_ctx // KV_BLOCK)
        xa = _out_proj(xa, mod_l, o_f, o_b, z_a, gdn_out_gain[l][None, :], out_b, out_c,
                       w_out[l].astype(BF16), post_gain[l][None, :], n_ctx_tiles)
    return xa[:, n_ctx:]
```

```python
import functools
import math

import jax
import jax.numpy as jnp
from jax import lax
from jax.experimental import pallas as pl
from jax.experimental.pallas import tpu as pltpu

F32 = jnp.float32
BF16 = jnp.bfloat16

D_MODEL = 1024
GRID_W = 64
NORM_EPS = 1e-6
ROPE_BASE = 10000.0

HEADS = 4
GDN_DK = 128
GDN_DV = 128
GDN_CHUNK = 64
CONV_W = 5
GDN_WIDTH = HEADS * GDN_DV
GDN_QK = HEADS * GDN_DK
GDN_CONV_CH = 2 * GDN_QK + GDN_WIDTH

MLA_Q_RANK = 256
MLA_KV_RANK = 128
MLA_NOPE = 64
MLA_ROPE = 32
MLA_V = 64
MLA_WIDTH = HEADS * MLA_V
MLA_SCALE = (MLA_NOPE + MLA_ROPE) ** -0.5

DIFF_QK = 32
DIFF_V = 64
DIFF_WIDTH = HEADS * DIFF_V
DIFF_QK_COLS = HEADS * 2 * DIFF_QK
DIFF_SCALE = DIFF_QK ** -0.5

LOG2E = 1.4426950408889634

LANES = 128
SUBLANES = 8
VMEM_LIMIT = 48 * 1024 * 1024

ROW_TILE = 256
KV_BLOCK = 256
GDN_STEP = 2 * GDN_CHUNK
HEAD_PAD = LANES

_O_QKV = 0
_O_ZA = _O_QKV + GDN_CONV_CH
_O_BL = _O_ZA + GDN_WIDTH
_O_AL = _O_BL + 2 * HEADS
_O_CQ = _O_AL + 2 * HEADS
_O_CKV = _O_CQ + MLA_Q_RANK
_O_KR = _O_CKV + MLA_KV_RANK
_O_ZB = _O_KR + MLA_ROPE
_O_DQ = _O_ZB + MLA_WIDTH
_O_DK = _O_DQ + DIFF_QK_COLS
_O_DV = _O_DK + DIFF_QK_COLS
_O_DZ = _O_DV + DIFF_WIDTH

_PACK = {}
_off = 0
for _name, _w in (("qkv", GDN_CONV_CH), ("za", GDN_WIDTH), ("ba", LANES), ("cq", MLA_Q_RANK),
                  ("ckv", MLA_KV_RANK), ("kr", 2 * LANES), ("zb", MLA_WIDTH),
                  ("dq", 2 * HEADS * HEAD_PAD), ("dk", 2 * HEADS * HEAD_PAD),
                  ("dv", DIFF_WIDTH), ("dz", DIFF_WIDTH)):
    _PACK[_name] = (_off, _w)
    _off += _w
PACK_COLS = _off


def _silu(x):
    return x * jax.nn.sigmoid(x)


def _rms(x, gain):
    return x * lax.rsqrt(jnp.mean(x * x, axis=-1, keepdims=True) + NORM_EPS) * gain


def _dot(a, b):
    return jnp.dot(a, b, preferred_element_type=F32)


def _dot_nt(a, b):
    return lax.dot_general(a, b, (((1,), (1,)), ((), ())), preferred_element_type=F32)


def _mod_kernel(c_ref, w_ref, b_ref, o_ref):
    a = _silu(c_ref[...])
    o_ref[0] = jnp.dot(a, w_ref[0], preferred_element_type=F32,
                       precision=lax.Precision.HIGHEST) + b_ref[0]


def _modulation(cc, ada_w, ada_b):
    depth = ada_w.shape[0]
    ncol = ada_w.shape[2]
    blk = D_MODEL
    return pl.pallas_call(
        _mod_kernel,
        grid=(depth, ncol // blk),
        in_specs=[
            pl.BlockSpec((SUBLANES, D_MODEL), lambda l, j: (0, 0)),
            pl.BlockSpec((1, D_MODEL, blk), lambda l, j: (l, 0, j)),
            pl.BlockSpec((1, 1, blk), lambda l, j: (l, 0, j)),
        ],
        out_specs=pl.BlockSpec((1, SUBLANES, blk), lambda l, j: (l, 0, j)),
        out_shape=jax.ShapeDtypeStruct((depth, SUBLANES, ncol), F32),
        compiler_params=pltpu.CompilerParams(
            dimension_semantics=("arbitrary", "arbitrary"), vmem_limit_bytes=VMEM_LIMIT),
        name="adaln_mod",
    )(cc, ada_w, ada_b.reshape(depth, 1, ncol))


def _in_kernel(x_ref, mod_ref, pg_ref, w_ref, alog_ref, dtb_ref, qg_ref, wuq_ref, kvg_ref, wukv_ref,
               cm_ref, sm_ref, cd_ref, sd_ref,
               qkv_ref, za_ref, gb_ref, qm_ref, km_ref, vtm_ref, zb_ref,
               qd_ref, kd_ref, vtd_ref, zd_ref):
    x = x_ref[0]
    mod = mod_ref[0, 0]
    shift = mod[0:1]
    scale = mod[1:2]
    h = (_rms(x, pg_ref[...]) * (1.0 + scale) + shift).astype(BF16)

    def proj(name):
        off, width = _PACK[name]
        return _dot(h, w_ref[:, off:off + width])

    qkv_ref[0] = proj("qkv")
    za_ref[0] = proj("za")
    ba = proj("ba")
    lane = lax.broadcasted_iota(jnp.int32, ba.shape, 1)
    beta = jax.nn.sigmoid(ba)
    zz = ba + dtb_ref[...]
    softplus = jnp.maximum(zz, 0.0) + jnp.log(1.0 + jnp.exp(-jnp.abs(zz)))
    gdec = -jnp.exp(alog_ref[...]) * softplus
    gb_ref[0] = jnp.where(lane < 2 * HEADS, beta, jnp.where(lane < 4 * HEADS, gdec, 0.0))

    cm = cm_ref[...]
    sm = sm_ref[...]
    nq = _rms(proj("cq"), qg_ref[...]).astype(BF16)
    qq = _dot(nq, wuq_ref[...])
    nkv = _rms(proj("ckv"), kvg_ref[...]).astype(BF16)
    kvv = _dot(nkv, wukv_ref[...])
    kr = proj("kr")
    krr = kr[:, :LANES] * cm + kr[:, LANES:] * sm
    half = HEADS * HEAD_PAD
    for hh in range(HEADS):
        sl = slice(hh * HEAD_PAD, (hh + 1) * HEAD_PAD)
        sw = slice(half + hh * HEAD_PAD, half + (hh + 1) * HEAD_PAD)
        qm_ref[0, :, sl] = ((qq[:, sl] * cm + qq[:, sw] * sm) * (MLA_SCALE * LOG2E)).astype(BF16)
        km_ref[0, :, sl] = (kvv[:, sl] + krr).astype(BF16)
    vtm_ref[0, 0] = kvv[:, half:half + MLA_WIDTH].T.astype(BF16)
    zb_ref[0] = proj("zb")

    cd = cd_ref[...]
    sd = sd_ref[...]
    dq = proj("dq")
    dk = proj("dk")
    for hh in range(HEADS):
        sl = slice(hh * HEAD_PAD, (hh + 1) * HEAD_PAD)
        sw = slice(half + hh * HEAD_PAD, half + (hh + 1) * HEAD_PAD)
        qd_ref[0, :, sl] = ((dq[:, sl] * cd + dq[:, sw] * sd) * (DIFF_SCALE * LOG2E)).astype(BF16)
        kd_ref[0, :, sl] = (dk[:, sl] * cd + dk[:, sw] * sd).astype(BF16)
    vtd_ref[0, 0] = proj("dv").T.astype(BF16)
    zd_ref[0] = proj("dz")


def _in_proj(xa, mod_l, pre_gain, w_pack, alog_row, dtb_row, q_gain, wuq, kv_gain, wukv, tabs, n_ctx_tiles):
    bsz, t_all, _ = xa.shape
    tm = ROW_TILE
    nt = t_all // tm
    full = lambda shape: pl.BlockSpec(shape, lambda b, i: (0,) * len(shape))
    row = lambda width: pl.BlockSpec((1, tm, width), lambda b, i: (b, i, 0))
    tab = pl.BlockSpec((tm, LANES), lambda b, i: (i, 0))
    vt = pl.BlockSpec((1, 1, MLA_WIDTH, tm), lambda b, i: (b, i, 0, 0))
    act = lambda width, dt: jax.ShapeDtypeStruct((bsz, t_all, width), dt)
    vts = jax.ShapeDtypeStruct((bsz, nt, MLA_WIDTH, tm), BF16)
    return pl.pallas_call(
        _in_kernel,
        grid=(bsz, nt),
        in_specs=[
            row(D_MODEL),
            pl.BlockSpec((1, 1, 3, D_MODEL), lambda b, i: (b, jnp.where(i < n_ctx_tiles, 0, 1), 0, 0)),
            full((1, D_MODEL)),
            full((D_MODEL, PACK_COLS)),
            full((1, LANES)), full((1, LANES)),
            full((1, MLA_Q_RANK)), full((MLA_Q_RANK, 2 * HEADS * HEAD_PAD)),
            full((1, MLA_KV_RANK)), full((MLA_KV_RANK, HEADS * HEAD_PAD + MLA_WIDTH)),
            tab, tab, tab, tab,
        ],
        out_specs=[row(GDN_CONV_CH), row(GDN_WIDTH), row(LANES),
                   row(HEADS * HEAD_PAD), row(HEADS * HEAD_PAD), vt, row(MLA_WIDTH),
                   row(HEADS * HEAD_PAD), row(HEADS * HEAD_PAD), vt, row(DIFF_WIDTH)],
        out_shape=[act(GDN_CONV_CH, F32), act(GDN_WIDTH, F32), act(LANES, F32),
                   act(HEADS * HEAD_PAD, BF16), act(HEADS * HEAD_PAD, BF16), vts, act(MLA_WIDTH, F32),
                   act(HEADS * HEAD_PAD, BF16), act(HEADS * HEAD_PAD, BF16), vts, act(DIFF_WIDTH, F32)],
        compiler_params=pltpu.CompilerParams(
            dimension_semantics=("parallel", "arbitrary"), vmem_limit_bytes=VMEM_LIMIT),
        name="in_proj",
    )(xa, mod_l, pre_gain, w_pack, alog_row, dtb_row, q_gain, wuq, kv_gain, wukv, *tabs)


def _gprep_kernel(n_ctx_tiles, cur_ref, prev_ref, next_ref, w_ref, o_ref, ext_ref):
    i = pl.program_id(1)
    nt = pl.num_programs(1)
    tm = cur_ref.shape[1]
    halo = CONV_W // 2
    has_prev = jnp.logical_and(i != 0, i != n_ctx_tiles)
    has_next = jnp.logical_and(i != n_ctx_tiles - 1, i != nt - 1)
    ext_ref[0:SUBLANES] = jnp.where(has_prev, prev_ref[0], 0.0)
    ext_ref[SUBLANES:SUBLANES + tm] = cur_ref[0]
    ext_ref[SUBLANES + tm:2 * SUBLANES + tm] = jnp.where(has_next, next_ref[0], 0.0)
    acc = None
    for j in range(CONV_W):
        term = ext_ref[pl.ds(SUBLANES - halo + j, tm), :] * w_ref[j:j + 1, :]
        acc = term if acc is None else acc + term
    y = _silu(acc)
    for hh in range(2 * HEADS):
        sl = slice(hh * GDN_DK, (hh + 1) * GDN_DK)
        v = y[:, sl]
        v = v * lax.rsqrt(jnp.sum(v * v, axis=-1, keepdims=True) + NORM_EPS)
        if hh < HEADS:
            v = v * (GDN_DK ** -0.5)
        o_ref[0, :, sl] = v
    o_ref[0, :, 2 * GDN_QK:] = y[:, 2 * GDN_QK:]


def _gdn_prepare(p_qkv, conv_w, n_ctx_tiles):
    bsz, t_all, ch = p_qkv.shape
    tm = ROW_TILE
    nt = t_all // tm
    per = tm // SUBLANES
    last = t_all // SUBLANES - 1
    return pl.pallas_call(
        functools.partial(_gprep_kernel, n_ctx_tiles),
        grid=(bsz, nt),
        in_specs=[
            pl.BlockSpec((1, tm, ch), lambda b, i: (b, i, 0)),
            pl.BlockSpec((1, SUBLANES, ch), lambda b, i: (b, jnp.maximum(i * per - 1, 0), 0)),
            pl.BlockSpec((1, SUBLANES, ch), lambda b, i: (b, jnp.minimum((i + 1) * per, last), 0)),
            pl.BlockSpec((CONV_W, ch), lambda b, i: (0, 0)),
        ],
        out_specs=pl.BlockSpec((1, tm, ch), lambda b, i: (b, i, 0)),
        out_shape=jax.ShapeDtypeStruct((bsz, t_all, ch), F32),
        scratch_shapes=[pltpu.VMEM((tm + 2 * SUBLANES, ch), F32)],
        compiler_params=pltpu.CompilerParams(
            dimension_semantics=("parallel", "arbitrary"), vmem_limit_bytes=VMEM_LIMIT),
        name="gdn_prep",
    )(p_qkv, p_qkv, p_qkv, conv_w)


def _mm(a, b):
    return _dot(a.astype(BF16), b.astype(BF16))


def _unit_lower_inverse(a_mat, eye, blk16):
    d = jnp.where(blk16, a_mat, 0.0)
    n = a_mat - d
    dinv = eye - d
    dk = d
    for _ in range(3):
        dk = _mm(dk, dk)
        dinv = dinv + _mm(dinv, dk)
    m = _mm(dinv, n)
    m2 = _mm(m, m)
    imm = eye - m
    minv = imm + _mm(imm, m2)
    return _mm(minv, dinv)


def _gscan_kernel(qf_ref, gf_ref, qb_ref, gb_ref, of_ref, ob_ref, s_ref):
    step = pl.program_id(1)

    @pl.when(step == 0)
    def _():
        s_ref[...] = jnp.zeros_like(s_ref)

    n = GDN_STEP
    c = GDN_CHUNK
    r = lax.broadcasted_iota(jnp.int32, (n, n), 0)
    col = lax.broadcasted_iota(jnp.int32, (n, n), 1)
    same = (r >= c) == (col >= c)
    eye = jnp.where(r == col, 1.0, 0.0).astype(F32)
    blk16 = (r // 16) == (col // 16)
    first_rows = r < c
    first_cols = col < c

    for d, (q_ref, g_ref, o_ref) in enumerate(((qf_ref, gf_ref, of_ref), (qb_ref, gb_ref, ob_ref))):
        if d == 0:
            incl = jnp.logical_and(same, r >= col)
            strict = jnp.logical_and(same, r > col)
            last_a, last_b = c - 1, n - 1
            order = (0, 1)
        else:
            incl = jnp.logical_and(same, r <= col)
            strict = jnp.logical_and(same, r < col)
            last_a, last_b = 0, c
            order = (1, 0)
        gbv = g_ref[0]
        gam = jnp.dot(jnp.where(incl, 1.0, 0.0).astype(F32), gbv, preferred_element_type=F32,
                      precision=lax.Precision.HIGHEST)
        gam_t = gam.T
        glast = jnp.where(first_rows, gam[last_a:last_a + 1, :], gam[last_b:last_b + 1, :])
        eg_all = jnp.exp(gam)
        kdf_all = jnp.exp(glast - gam)
        egl_all = jnp.exp(glast)
        for hh in range(HEADS):
            cb = d * HEADS + hh
            cg = 2 * HEADS + d * HEADS + hh
            chain = d * HEADS + hh
            q = q_ref[0, :, hh * GDN_DK:(hh + 1) * GDN_DK]
            k = q_ref[0, :, GDN_QK + hh * GDN_DK:GDN_QK + (hh + 1) * GDN_DK]
            v = q_ref[0, :, 2 * GDN_QK + hh * GDN_DV:2 * GDN_QK + (hh + 1) * GDN_DV]
            beta = gbv[:, cb:cb + 1]
            diff = gam[:, cg:cg + 1] - gam_t[cg:cg + 1, :]
            dec = jnp.where(incl, jnp.exp(jnp.where(incl, diff, 0.0)), 0.0)
            eg = eg_all[:, cg:cg + 1]
            kb = k * beta
            kk_qk = _dot_nt(jnp.concatenate([kb, q], axis=0).astype(BF16), k.astype(BF16))
            a_mat = jnp.where(strict, kk_qk[:n] * dec, 0.0)
            qkm = jnp.where(incl, kk_qk[n:] * dec, 0.0)
            t_mat = _unit_lower_inverse(a_mat, eye, blk16)
            uw = _mm(t_mat, jnp.concatenate([v * beta, kb * eg], axis=1))
            u = uw[:, :GDN_DV]
            w = uw[:, GDN_DV:]
            qd = q * eg
            kd_t = (k * kdf_all[:, cg:cg + 1]).T
            s = s_ref[chain]
            vn = [None, None]
            qs = [None, None]
            for ci in order:
                rows = slice(ci * c, (ci + 1) * c)
                wsqs = _mm(jnp.concatenate([w[rows], qd[rows]], axis=0), s)
                vn[ci] = u[rows] - wsqs[:c]
                qs[ci] = wsqs[c:]
                zero = jnp.zeros_like(vn[ci])
                vfull = jnp.concatenate([vn[ci], zero] if ci == 0 else [zero, vn[ci]], axis=0)
                kd_c = jnp.where(first_cols if ci == 0 else jnp.logical_not(first_cols), kd_t, 0.0)
                lrow = last_a if ci == 0 else last_b
                s = s * egl_all[lrow:lrow + 1, cg:cg + 1] + _mm(kd_c, vfull)
            s_ref[chain] = s
            o = jnp.concatenate(qs, axis=0) + _mm(qkm, jnp.concatenate(vn, axis=0))
            o_ref[0, :, hh * GDN_DV:(hh + 1) * GDN_DV] = o


def _gdn_scan(qkv_n, gb, n_ctx_steps):
    bsz, t_all, ch = qkv_n.shape
    n = GDN_STEP
    nsteps = t_all // n

    def bwd(i):
        return jnp.where(i < n_ctx_steps, n_ctx_steps - 1 - i, nsteps - 1 - (i - n_ctx_steps))

    out = jax.ShapeDtypeStruct((bsz, t_all, GDN_WIDTH), F32)
    return pl.pallas_call(
        _gscan_kernel,
        grid=(bsz, nsteps),
        in_specs=[
            pl.BlockSpec((1, n, ch), lambda b, i: (b, i, 0)),
            pl.BlockSpec((1, n, LANES), lambda b, i: (b, i, 0)),
            pl.BlockSpec((1, n, ch), lambda b, i: (b, bwd(i), 0)),
            pl.BlockSpec((1, n, LANES), lambda b, i: (b, bwd(i), 0)),
        ],
        out_specs=[
            pl.BlockSpec((1, n, GDN_WIDTH), lambda b, i: (b, i, 0)),
            pl.BlockSpec((1, n, GDN_WIDTH), lambda b, i: (b, bwd(i), 0)),
        ],
        out_shape=[out, out],
        scratch_shapes=[pltpu.VMEM((2 * HEADS, GDN_DK, GDN_DV), F32)],
        compiler_params=pltpu.CompilerParams(
            dimension_semantics=("parallel", "arbitrary"), vmem_limit_bytes=VMEM_LIMIT),
        name="gdn_scan",
    )(qkv_n, gb, qkv_n, gb)


def _attn_kernel(n_maps, n_ctx_blocks, group, lam_init, q_ref, k_ref, vt_ref, z_ref, gain_ref, lam_ref,
                 o_ref, m_ref, l_ref, acc_ref, s0_ref, s1_ref):
    i = pl.program_id(1)
    tq = q_ref.shape[1]
    nblk = vt_ref.shape[1]
    dv = vt_ref.shape[2] // HEADS
    n_groups = nblk // group

    m_ref[...] = jnp.full_like(m_ref, -1e30)
    l_ref[...] = jnp.zeros_like(l_ref)
    acc_ref[...] = jnp.zeros_like(acc_ref)

    lane = lax.broadcasted_iota(jnp.int32, (tq, HEAD_PAD), 1)
    qs = []
    for hh in range(HEADS):
        qh = q_ref[0, :, hh * HEAD_PAD:(hh + 1) * HEAD_PAD]
        for mm in range(n_maps):
            if n_maps == 1:
                qs.append(qh)
            else:
                keep = jnp.logical_and(lane >= mm * DIFF_QK, lane < (mm + 1) * DIFF_QK)
                qs.append(jnp.where(keep, qh, jnp.zeros_like(qh)))

    def scores(buf, first_blk, nb):
        rows = nb * KV_BLOCK
        start = pl.multiple_of(first_blk * KV_BLOCK, KV_BLOCK)
        for hh in range(HEADS):
            kk = k_ref[0, pl.ds(start, rows), hh * HEAD_PAD:(hh + 1) * HEAD_PAD]
            for mm in range(n_maps):
                idx = hh * n_maps + mm
                buf[idx, 0:rows, :] = _dot_nt(kk, qs[idx])

    def update(buf, first_blk, nb):
        rows = nb * KV_BLOCK
        for hh in range(HEADS):
            vts = [vt_ref[0, first_blk + b, hh * dv:(hh + 1) * dv, :] for b in range(nb)]
            for mm in range(n_maps):
                idx = hh * n_maps + mm
                m_prev = m_ref[idx]
                m_new = jnp.maximum(m_prev, jnp.max(buf[idx, 0:rows, :], axis=0, keepdims=True))
                alpha = jnp.exp2(m_prev - m_new)
                p = jnp.exp2(buf[idx, 0:rows, :] - m_new)
                l_ref[idx] = alpha * l_ref[idx] + jnp.sum(p, axis=0, keepdims=True)
                pb = p.astype(BF16)
                pv = functools.reduce(jnp.add, [_dot(vts[b], pb[b * KV_BLOCK:(b + 1) * KV_BLOCK])
                                                for b in range(nb)])
                acc_ref[idx] = alpha * acc_ref[idx] + pv
                m_ref[idx] = m_new

    @pl.when(i < n_ctx_blocks)
    def _():
        scores(s0_ref, 0, n_ctx_blocks)
        update(s0_ref, 0, n_ctx_blocks)

    @pl.when(i >= n_ctx_blocks)
    def _():
        def body(g, carry):
            scores(s0_ref, g * group, group)
            update(s0_ref, g * group, group)
            return carry

        lax.fori_loop(0, n_groups, body, 0)

    outs = []
    for hh in range(HEADS):
        if n_maps == 1:
            o_t = acc_ref[hh] / l_ref[hh]
        else:
            lam = lam_ref[...]
            lq = jnp.sum(lam[0:1] * lam[1:2], axis=-1, keepdims=True)
            lk = jnp.sum(lam[2:3] * lam[3:4], axis=-1, keepdims=True)
            lam_full = jnp.exp(lq) - jnp.exp(lk) + lam_init
            o_t = acc_ref[2 * hh] / l_ref[2 * hh] - lam_full * (acc_ref[2 * hh + 1] / l_ref[2 * hh + 1])
            ms = jnp.mean(o_t * o_t, axis=0, keepdims=True)
            o_t = o_t * lax.rsqrt(ms + NORM_EPS) * gain_ref[...] * (1.0 - lam_init)
        outs.append(o_t)
    o = jnp.concatenate(outs, axis=0).T
    o_ref[0] = o * _silu(z_ref[0])


def _attention(n_maps, lam_init, q, k, vt, z, gain_col, lam, n_ctx_blocks):
    bsz, t_all, width = q.shape
    tq = ROW_TILE
    nq = t_all // tq
    nblk = vt.shape[1]
    hv = vt.shape[2]
    nhm = HEADS * n_maps
    group = next(g for g in (3, 2, 1) if nblk % g == 0)
    stage_rows = max(group, n_ctx_blocks) * KV_BLOCK
    return pl.pallas_call(
        functools.partial(_attn_kernel, n_maps, n_ctx_blocks, group, lam_init),
        grid=(bsz, nq),
        in_specs=[
            pl.BlockSpec((1, tq, width), lambda b, i: (b, i, 0)),
            pl.BlockSpec((1, t_all, width), lambda b, i: (b, 0, 0)),
            pl.BlockSpec((1, nblk, hv, KV_BLOCK), lambda b, i: (b, 0, 0, 0)),
            pl.BlockSpec((1, tq, hv), lambda b, i: (b, i, 0)),
            pl.BlockSpec(gain_col.shape, lambda b, i: (0, 0)),
            pl.BlockSpec(lam.shape, lambda b, i: (0, 0)),
        ],
        out_specs=pl.BlockSpec((1, tq, hv), lambda b, i: (b, i, 0)),
        out_shape=jax.ShapeDtypeStruct((bsz, t_all, hv), F32),
        scratch_shapes=[pltpu.VMEM((nhm, 1, tq), F32), pltpu.VMEM((nhm, 1, tq), F32),
                        pltpu.VMEM((nhm, hv // HEADS, tq), F32),
                        pltpu.VMEM((nhm, stage_rows, tq), F32), pltpu.VMEM((nhm, stage_rows, tq), F32)],
        compiler_params=pltpu.CompilerParams(
            dimension_semantics=("parallel", "arbitrary"), vmem_limit_bytes=VMEM_LIMIT),
        name="mla_attn" if n_maps == 1 else "diff_attn",
    )(q, k, vt, z, gain_col, lam)


def _out_kernel(x_ref, mod_ref, of_ref, ob_ref, za_ref, ag_ref, b_ref, c_ref, w_ref, pg_ref, o_ref):
    o_a = of_ref[0] + ob_ref[0]
    za = za_ref[0]
    parts = []
    for hh in range(HEADS):
        sl = slice(hh * GDN_DV, (hh + 1) * GDN_DV)
        parts.append(_rms(o_a[:, sl], ag_ref[...]) * _silu(za[:, sl]))
    parts.append(b_ref[0])
    parts.append(c_ref[0])
    cat = jnp.concatenate(parts, axis=1).astype(BF16)
    y = _dot(cat, w_ref[...])
    gate = mod_ref[0, 0][2:3]
    o_ref[0] = x_ref[0] + gate * _rms(y, pg_ref[...])


def _out_proj(xa, mod_l, o_f, o_b, z_a, a_gain, out_b, out_c, w_out, post_gain, n_ctx_tiles):
    bsz, t_all, _ = xa.shape
    tm = ROW_TILE
    nt = t_all // tm
    row = lambda width: pl.BlockSpec((1, tm, width), lambda b, i: (b, i, 0))
    full = lambda shape: pl.BlockSpec(shape, lambda b, i: (0,) * len(shape))
    return pl.pallas_call(
        _out_kernel,
        grid=(bsz, nt),
        in_specs=[
            row(D_MODEL),
            pl.BlockSpec((1, 1, 3, D_MODEL), lambda b, i: (b, jnp.where(i < n_ctx_tiles, 0, 1), 0, 0)),
            row(GDN_WIDTH), row(GDN_WIDTH), row(GDN_WIDTH), full((1, GDN_DV)),
            row(MLA_WIDTH), row(DIFF_WIDTH), full((D_MODEL, D_MODEL)), full((1, D_MODEL)),
        ],
        out_specs=row(D_MODEL),
        out_shape=jax.ShapeDtypeStruct(xa.shape, F32),
        compiler_params=pltpu.CompilerParams(
            dimension_semantics=("parallel", "arbitrary"), vmem_limit_bytes=VMEM_LIMIT),
        name="out_proj",
    )(xa, mod_l, o_f, o_b, z_a, a_gain, out_b, out_c, w_out, post_gain)


def _swap_halves(w):
    half = w.shape[-1] // 2
    return jnp.concatenate([w[..., half:], w[..., :half]], axis=-1)


def _pack_w_in(w):
    d = w.shape[0]
    z = lambda n: jnp.zeros((d, n), w.dtype)
    kr = w[:, _O_KR:_O_KR + MLA_ROPE]
    pad_r = HEAD_PAD - MLA_NOPE - MLA_ROPE
    kr_group = jnp.concatenate([z(MLA_NOPE), kr, z(pad_r), z(MLA_NOPE), _swap_halves(kr), z(pad_r)], axis=1)

    def diff_group(off):
        plain, swapped = [], []
        for hh in range(HEADS):
            m0 = w[:, off + hh * 2 * DIFF_QK:off + hh * 2 * DIFF_QK + DIFF_QK]
            m1 = w[:, off + hh * 2 * DIFF_QK + DIFF_QK:off + (hh + 1) * 2 * DIFF_QK]
            plain += [m0, m1, z(HEAD_PAD - 2 * DIFF_QK)]
            swapped += [_swap_halves(m0), _swap_halves(m1), z(HEAD_PAD - 2 * DIFF_QK)]
        return jnp.concatenate(plain + swapped, axis=1)

    groups = [
        w[:, _O_QKV:_O_QKV + GDN_CONV_CH],
        w[:, _O_ZA:_O_ZA + GDN_WIDTH],
        jnp.concatenate([w[:, _O_BL:_O_BL + 4 * HEADS], z(LANES - 4 * HEADS)], axis=1),
        w[:, _O_CQ:_O_CQ + MLA_Q_RANK],
        w[:, _O_CKV:_O_CKV + MLA_KV_RANK],
        kr_group,
        w[:, _O_ZB:_O_ZB + MLA_WIDTH],
        diff_group(_O_DQ),
        diff_group(_O_DK),
        w[:, _O_DV:_O_DV + DIFF_WIDTH],
        w[:, _O_DZ:_O_DZ + DIFF_WIDTH],
    ]
    return jnp.concatenate(groups, axis=1).astype(BF16)


def _pack_w_uq(w):
    d = w.shape[0]
    z = lambda n: jnp.zeros((d, n), w.dtype)
    per = MLA_NOPE + MLA_ROPE
    pad_r = HEAD_PAD - per
    plain, swapped = [], []
    for hh in range(HEADS):
        nope = w[:, hh * per:hh * per + MLA_NOPE]
        rope = w[:, hh * per + MLA_NOPE:(hh + 1) * per]
        plain += [nope, rope, z(pad_r)]
        swapped += [z(MLA_NOPE), _swap_halves(rope), z(pad_r)]
    return jnp.concatenate(plain + swapped, axis=1).astype(BF16)


def _pack_w_ukv(w):
    d = w.shape[0]
    z = lambda n: jnp.zeros((d, n), w.dtype)
    per = MLA_NOPE + MLA_V
    keys, vals = [], []
    for hh in range(HEADS):
        keys += [w[:, hh * per:hh * per + MLA_NOPE], z(HEAD_PAD - MLA_NOPE)]
        vals.append(w[:, hh * per + MLA_NOPE:(hh + 1) * per])
    return jnp.concatenate(keys + vals, axis=1).astype(BF16)


def _rope_tables(seq, ctx):
    n_rows = seq // GRID_W
    row = jnp.repeat(jnp.arange(n_rows), GRID_W).astype(F32)
    colp = jnp.tile(jnp.arange(GRID_W), n_rows).astype(F32)
    d_axis = MLA_ROPE // 2
    inv = 1.0 / (ROPE_BASE ** (jnp.arange(0, d_axis, 2, dtype=F32) / d_axis))
    ang = jnp.concatenate([row[:, None] * inv, colp[:, None] * inv], axis=-1)
    cos = jnp.concatenate([jnp.ones((ctx, ang.shape[1]), F32), jnp.cos(ang)], axis=0)
    sin = jnp.concatenate([jnp.zeros((ctx, ang.shape[1]), F32), jnp.sin(ang)], axis=0)
    t_all = seq + ctx
    ones = lambda n: jnp.ones((t_all, n), F32)
    zeros = lambda n: jnp.zeros((t_all, n), F32)
    pad_r = HEAD_PAD - MLA_NOPE - MLA_ROPE
    cos_m = jnp.concatenate([ones(MLA_NOPE), cos, cos, ones(pad_r)], axis=1)
    sin_m = jnp.concatenate([zeros(MLA_NOPE), -sin, sin, zeros(pad_r)], axis=1)
    pad_d = HEAD_PAD - 2 * DIFF_QK
    cos_d = jnp.concatenate([cos, cos, cos, cos, ones(pad_d)], axis=1)
    sin_d = jnp.concatenate([-sin, sin, -sin, sin, zeros(pad_d)], axis=1)
    return cos_m, sin_m, cos_d, sin_d


def _lane_row(p):
    flat = p.reshape(1, 2 * HEADS).astype(F32)
    return jnp.concatenate([jnp.zeros((1, 2 * HEADS), F32), flat,
                            jnp.zeros((1, LANES - 4 * HEADS), F32)], axis=1)


def kernel(x, c, ctx, c_ctx, ada_w, ada_b, pre_gain, post_gain, w_in, gdn_conv, gdn_a_log, gdn_dt_bias,
           gdn_out_gain, mla_q_gain, mla_w_uq, mla_kv_gain, mla_w_ukv, diff_lambda, diff_sub_gain, w_out):
    bsz, seq, _ = x.shape
    n_ctx = ctx.shape[1]
    depth = ada_w.shape[0]
    assert n_ctx % ROW_TILE == 0 and seq % ROW_TILE == 0 and seq % GRID_W == 0 and ROW_TILE == KV_BLOCK
    assert bsz + 1 <= SUBLANES
    n_ctx_tiles = n_ctx // ROW_TILE

    xa = jnp.concatenate([ctx, x], axis=1)
    cc = jnp.concatenate([c, c_ctx[None, :], jnp.zeros((SUBLANES - bsz - 1, D_MODEL), F32)], axis=0)
    mod = _modulation(cc, ada_w, ada_b)
    tabs = _rope_tables(seq, n_ctx)

    for l in range(depth):
        lam_init = 0.8 - 0.6 * math.exp(-0.3 * l)
        mod_rows = mod[l].reshape(SUBLANES, 3, D_MODEL)
        mod_l = jnp.stack([jnp.broadcast_to(mod_rows[bsz], (bsz, 3, D_MODEL)), mod_rows[:bsz]], axis=1)
        (p_qkv, z_a, gb, q_m, k_m, vt_m, z_b, q_d, k_d, vt_d, z_d) = _in_proj(
            xa, mod_l, pre_gain[l][None, :], _pack_w_in(w_in[l]),
            _lane_row(gdn_a_log[l]), _lane_row(gdn_dt_bias[l]),
            mla_q_gain[l][None, :], _pack_w_uq(mla_w_uq[l]),
            mla_kv_gain[l][None, :], _pack_w_ukv(mla_w_ukv[l]), tabs, n_ctx_tiles)
        qkv_n = _gdn_prepare(p_qkv, gdn_conv[l], n_ctx_tiles)
        o_f, o_b = _gdn_scan(qkv_n, gb, n_ctx // GDN_STEP)
        gain_col = diff_sub_gain[l][:, None]
        out_b = _attention(1, lam_init, q_m, k_m, vt_m, z_b, gain_col, diff_lambda[l], n_ctx // KV_BLOCK)
        out_c = _attention(2, lam_init, q_d, k_d, vt_d, z_d, gain_col, diff_lambda[l], n_ctx // KV_BLOCK)
        xa = _out_proj(xa, mod_l, o_f, o_b, z_a, gdn_out_gain[l][None, :], out_b, out_c,
                       w_out[l].astype(BF16), post_gain[l][None, :], n_ctx_tiles)
    return xa[:, n_ctx:]
```

```python
import functools
import math

import jax
import jax.numpy as jnp
from jax import lax
from jax.experimental import pallas as pl
from jax.experimental.pallas import tpu as pltpu

F32 = jnp.float32
BF16 = jnp.bfloat16

D_MODEL = 1024
GRID_W = 64
NORM_EPS = 1e-6
ROPE_BASE = 10000.0

HEADS = 4
GDN_DK = 128
GDN_DV = 128
GDN_CHUNK = 64
CONV_W = 5
GDN_WIDTH = HEADS * GDN_DV
GDN_QK = HEADS * GDN_DK
GDN_CONV_CH = 2 * GDN_QK + GDN_WIDTH

MLA_Q_RANK = 256
MLA_KV_RANK = 128
MLA_NOPE = 64
MLA_ROPE = 32
MLA_V = 64
MLA_WIDTH = HEADS * MLA_V
MLA_SCALE = (MLA_NOPE + MLA_ROPE) ** -0.5

DIFF_QK = 32
DIFF_V = 64
DIFF_WIDTH = HEADS * DIFF_V
DIFF_QK_COLS = HEADS * 2 * DIFF_QK
DIFF_SCALE = DIFF_QK ** -0.5

LOG2E = 1.4426950408889634

LANES = 128
SUBLANES = 8
VMEM_LIMIT = 48 * 1024 * 1024

ROW_TILE = 256
KV_BLOCK = 256
GDN_STEP = 2 * GDN_CHUNK
HEAD_PAD = LANES
VT_PAD = 16
VT_ROWS = HEADS * (MLA_V + VT_PAD)

_O_QKV = 0
_O_ZA = _O_QKV + GDN_CONV_CH
_O_BL = _O_ZA + GDN_WIDTH
_O_AL = _O_BL + 2 * HEADS
_O_CQ = _O_AL + 2 * HEADS
_O_CKV = _O_CQ + MLA_Q_RANK
_O_KR = _O_CKV + MLA_KV_RANK
_O_ZB = _O_KR + MLA_ROPE
_O_DQ = _O_ZB + MLA_WIDTH
_O_DK = _O_DQ + DIFF_QK_COLS
_O_DV = _O_DK + DIFF_QK_COLS
_O_DZ = _O_DV + DIFF_WIDTH

_PACK = {}
_off = 0
for _name, _w in (("qkv", GDN_CONV_CH), ("za", GDN_WIDTH), ("ba", LANES), ("cq", MLA_Q_RANK),
                  ("ckv", MLA_KV_RANK), ("kr", 2 * LANES), ("zb", MLA_WIDTH),
                  ("dq", 2 * HEADS * HEAD_PAD), ("dk", 2 * HEADS * HEAD_PAD),
                  ("dv", DIFF_WIDTH), ("dz", DIFF_WIDTH)):
    _PACK[_name] = (_off, _w)
    _off += _w
PACK_COLS = _off


def _silu(x):
    return x * jax.nn.sigmoid(x)


def _rms(x, gain):
    return x * lax.rsqrt(jnp.mean(x * x, axis=-1, keepdims=True) + NORM_EPS) * gain


def _dot(a, b):
    return jnp.dot(a, b, preferred_element_type=F32)


def _dot_nt(a, b):
    return lax.dot_general(a, b, (((1,), (1,)), ((), ())), preferred_element_type=F32)


def _mod_kernel(c_ref, w_ref, b_ref, o_ref):
    a = _silu(c_ref[...])
    o_ref[0] = jnp.dot(a, w_ref[0], preferred_element_type=F32,
                       precision=lax.Precision.HIGHEST) + b_ref[0]


def _modulation(cc, ada_w, ada_b):
    depth = ada_w.shape[0]
    ncol = ada_w.shape[2]
    blk = D_MODEL
    return pl.pallas_call(
        _mod_kernel,
        grid=(depth, ncol // blk),
        in_specs=[
            pl.BlockSpec((SUBLANES, D_MODEL), lambda l, j: (0, 0)),
            pl.BlockSpec((1, D_MODEL, blk), lambda l, j: (l, 0, j)),
            pl.BlockSpec((1, 1, blk), lambda l, j: (l, 0, j)),
        ],
        out_specs=pl.BlockSpec((1, SUBLANES, blk), lambda l, j: (l, 0, j)),
        out_shape=jax.ShapeDtypeStruct((depth, SUBLANES, ncol), F32),
        compiler_params=pltpu.CompilerParams(
            dimension_semantics=("arbitrary", "arbitrary"), vmem_limit_bytes=VMEM_LIMIT),
        name="adaln_mod",
    )(cc, ada_w, ada_b.reshape(depth, 1, ncol))


def _augmented_vt(v):
    rows = v.shape[0]
    dv = v.shape[1] // HEADS
    vt = v.T
    ones_blk = jnp.where(lax.broadcasted_iota(jnp.int32, (VT_PAD, rows), 0) == 0, 1.0, 0.0).astype(F32)
    parts = []
    for hh in range(HEADS):
        parts += [vt[hh * dv:(hh + 1) * dv], ones_blk]
    return jnp.concatenate(parts, axis=0).astype(BF16)


def _in_kernel(x_ref, mod_ref, pg_ref, w_ref, alog_ref, dtb_ref, qg_ref, wuq_ref, kvg_ref, wukv_ref,
               cm_ref, sm_ref, cd_ref, sd_ref,
               qkv_ref, za_ref, gb_ref, qm_ref, km_ref, vtm_ref, zb_ref,
               qd_ref, kd_ref, vtd_ref, zd_ref):
    x = x_ref[0]
    mod = mod_ref[0, 0]
    shift = mod[0:1]
    scale = mod[1:2]
    h = (_rms(x, pg_ref[...]) * (1.0 + scale) + shift).astype(BF16)

    def proj(name):
        off, width = _PACK[name]
        return _dot(h, w_ref[:, off:off + width])

    qkv_ref[0] = proj("qkv")
    za_ref[0] = proj("za")
    ba = proj("ba")
    lane = lax.broadcasted_iota(jnp.int32, ba.shape, 1)
    beta = jax.nn.sigmoid(ba)
    zz = ba + dtb_ref[...]
    softplus = jnp.maximum(zz, 0.0) + jnp.log(1.0 + jnp.exp(-jnp.abs(zz)))
    gdec = -jnp.exp(alog_ref[...]) * softplus
    gb_ref[0] = jnp.where(lane < 2 * HEADS, beta, jnp.where(lane < 4 * HEADS, gdec, 0.0))

    cm = cm_ref[...]
    sm = sm_ref[...]
    nq = _rms(proj("cq"), qg_ref[...]).astype(BF16)
    qq = _dot(nq, wuq_ref[...])
    nkv = _rms(proj("ckv"), kvg_ref[...]).astype(BF16)
    kvv = _dot(nkv, wukv_ref[...])
    kr = proj("kr")
    krr = kr[:, :LANES] * cm + kr[:, LANES:] * sm
    half = HEADS * HEAD_PAD
    for hh in range(HEADS):
        sl = slice(hh * HEAD_PAD, (hh + 1) * HEAD_PAD)
        sw = slice(half + hh * HEAD_PAD, half + (hh + 1) * HEAD_PAD)
        qm_ref[0, :, sl] = ((qq[:, sl] * cm + qq[:, sw] * sm) * (MLA_SCALE * LOG2E)).astype(BF16)
        km_ref[0, :, sl] = (kvv[:, sl] + krr).astype(BF16)
    vtm_ref[0, 0] = _augmented_vt(kvv[:, half:half + MLA_WIDTH])
    zb_ref[0] = proj("zb")

    cd = cd_ref[...]
    sd = sd_ref[...]
    dq = proj("dq")
    dk = proj("dk")
    for hh in range(HEADS):
        sl = slice(hh * HEAD_PAD, (hh + 1) * HEAD_PAD)
        sw = slice(half + hh * HEAD_PAD, half + (hh + 1) * HEAD_PAD)
        qd_ref[0, :, sl] = ((dq[:, sl] * cd + dq[:, sw] * sd) * (DIFF_SCALE * LOG2E)).astype(BF16)
        kd_ref[0, :, sl] = (dk[:, sl] * cd + dk[:, sw] * sd).astype(BF16)
    vtd_ref[0, 0] = _augmented_vt(proj("dv"))
    zd_ref[0] = proj("dz")


def _in_proj(xa, mod_l, pre_gain, w_pack, alog_row, dtb_row, q_gain, wuq, kv_gain, wukv, tabs, n_ctx_tiles):
    bsz, t_all, _ = xa.shape
    tm = ROW_TILE
    nt = t_all // tm
    full = lambda shape: pl.BlockSpec(shape, lambda b, i: (0,) * len(shape))
    row = lambda width: pl.BlockSpec((1, tm, width), lambda b, i: (b, i, 0))
    tab = pl.BlockSpec((tm, LANES), lambda b, i: (i, 0))
    vt = pl.BlockSpec((1, 1, VT_ROWS, tm), lambda b, i: (b, i, 0, 0))
    act = lambda width, dt: jax.ShapeDtypeStruct((bsz, t_all, width), dt)
    vts = jax.ShapeDtypeStruct((bsz, nt, VT_ROWS, tm), BF16)
    return pl.pallas_call(
        _in_kernel,
        grid=(bsz, nt),
        in_specs=[
            row(D_MODEL),
            pl.BlockSpec((1, 1, 3, D_MODEL), lambda b, i: (b, jnp.where(i < n_ctx_tiles, 0, 1), 0, 0)),
            full((1, D_MODEL)),
            full((D_MODEL, PACK_COLS)),
            full((1, LANES)), full((1, LANES)),
            full((1, MLA_Q_RANK)), full((MLA_Q_RANK, 2 * HEADS * HEAD_PAD)),
            full((1, MLA_KV_RANK)), full((MLA_KV_RANK, HEADS * HEAD_PAD + MLA_WIDTH)),
            tab, tab, tab, tab,
        ],
        out_specs=[row(GDN_CONV_CH), row(GDN_WIDTH), row(LANES),
                   row(HEADS * HEAD_PAD), row(HEADS * HEAD_PAD), vt, row(MLA_WIDTH),
                   row(HEADS * HEAD_PAD), row(HEADS * HEAD_PAD), vt, row(DIFF_WIDTH)],
        out_shape=[act(GDN_CONV_CH, F32), act(GDN_WIDTH, F32), act(LANES, F32),
                   act(HEADS * HEAD_PAD, BF16), act(HEADS * HEAD_PAD, BF16), vts, act(MLA_WIDTH, F32),
                   act(HEADS * HEAD_PAD, BF16), act(HEADS * HEAD_PAD, BF16), vts, act(DIFF_WIDTH, F32)],
        compiler_params=pltpu.CompilerParams(
            dimension_semantics=("parallel", "arbitrary"), vmem_limit_bytes=VMEM_LIMIT),
        name="in_proj",
    )(xa, mod_l, pre_gain, w_pack, alog_row, dtb_row, q_gain, wuq, kv_gain, wukv, *tabs)


def _gprep_kernel(n_ctx_tiles, cur_ref, prev_ref, next_ref, w_ref, o_ref, ext_ref):
    i = pl.program_id(1)
    nt = pl.num_programs(1)
    tm = cur_ref.shape[1]
    halo = CONV_W // 2
    has_prev = jnp.logical_and(i != 0, i != n_ctx_tiles)
    has_next = jnp.logical_and(i != n_ctx_tiles - 1, i != nt - 1)
    ext_ref[0:SUBLANES] = jnp.where(has_prev, prev_ref[0], 0.0)
    ext_ref[SUBLANES:SUBLANES + tm] = cur_ref[0]
    ext_ref[SUBLANES + tm:2 * SUBLANES + tm] = jnp.where(has_next, next_ref[0], 0.0)
    acc = None
    for j in range(CONV_W):
        term = ext_ref[pl.ds(SUBLANES - halo + j, tm), :] * w_ref[j:j + 1, :]
        acc = term if acc is None else acc + term
    y = _silu(acc)
    for hh in range(2 * HEADS):
        sl = slice(hh * GDN_DK, (hh + 1) * GDN_DK)
        v = y[:, sl]
        v = v * lax.rsqrt(jnp.sum(v * v, axis=-1, keepdims=True) + NORM_EPS)
        if hh < HEADS:
            v = v * (GDN_DK ** -0.5)
        o_ref[0, :, sl] = v
    o_ref[0, :, 2 * GDN_QK:] = y[:, 2 * GDN_QK:]


def _gdn_prepare(p_qkv, conv_w, n_ctx_tiles):
    bsz, t_all, ch = p_qkv.shape
    tm = ROW_TILE
    nt = t_all // tm
    per = tm // SUBLANES
    last = t_all // SUBLANES - 1
    return pl.pallas_call(
        functools.partial(_gprep_kernel, n_ctx_tiles),
        grid=(bsz, nt),
        in_specs=[
            pl.BlockSpec((1, tm, ch), lambda b, i: (b, i, 0)),
            pl.BlockSpec((1, SUBLANES, ch), lambda b, i: (b, jnp.maximum(i * per - 1, 0), 0)),
            pl.BlockSpec((1, SUBLANES, ch), lambda b, i: (b, jnp.minimum((i + 1) * per, last), 0)),
            pl.BlockSpec((CONV_W, ch), lambda b, i: (0, 0)),
        ],
        out_specs=pl.BlockSpec((1, tm, ch), lambda b, i: (b, i, 0)),
        out_shape=jax.ShapeDtypeStruct((bsz, t_all, ch), F32),
        scratch_shapes=[pltpu.VMEM((tm + 2 * SUBLANES, ch), F32)],
        compiler_params=pltpu.CompilerParams(
            dimension_semantics=("parallel", "arbitrary"), vmem_limit_bytes=VMEM_LIMIT),
        name="gdn_prep",
    )(p_qkv, p_qkv, p_qkv, conv_w)


def _mm(a, b):
    return _dot(a.astype(BF16), b.astype(BF16))


def _pair_bd(x):
    n = x.shape[0]
    xb = x.astype(BF16)
    z = jnp.zeros((n, n), BF16)
    return jnp.concatenate([jnp.concatenate([xb[:, :n], z], axis=1),
                            jnp.concatenate([z, xb[:, n:]], axis=1)], axis=0)


def _pmm(a, b):
    return _dot(a.astype(BF16), _pair_bd(b))


def _gscan_kernel(qf_ref, gf_ref, qb_ref, gb_ref, of_ref, ob_ref, s_ref):
    step = pl.program_id(1)

    @pl.when(step == 0)
    def _():
        s_ref[...] = jnp.zeros_like(s_ref)

    n = GDN_STEP
    c = GDN_CHUNK
    r = lax.broadcasted_iota(jnp.int32, (n, 2 * n), 0)
    col = lax.broadcasted_iota(jnp.int32, (n, 2 * n), 1) & (n - 1)
    same = (r >= c) == (col >= c)
    eye = jnp.where(r == col, 1.0, 0.0).astype(F32)
    blk16 = (r // 16) == (col // 16)
    first_rows = r < c
    first_cols = col < c
    zero_half = jnp.zeros((c, 2 * n), F32)

    def lane_pair(x0, x1):
        return jnp.concatenate([jnp.broadcast_to(x0, (n, n)), jnp.broadcast_to(x1, (n, n))], axis=1)

    chains = []
    for d, (q_ref, g_ref) in enumerate(((qf_ref, gf_ref), (qb_ref, gb_ref))):
        if d == 0:
            incl = jnp.logical_and(same, r >= col)
            strict = jnp.logical_and(same, r > col)
            last = (c - 1, n - 1)
            order = (0, 1)
        else:
            incl = jnp.logical_and(same, r <= col)
            strict = jnp.logical_and(same, r < col)
            last = (0, c)
            order = (1, 0)
        gbv = g_ref[0]
        gam = jnp.dot(jnp.where(incl[:, :n], 1.0, 0.0).astype(F32), gbv, preferred_element_type=F32,
                      precision=lax.Precision.HIGHEST)
        gam_t = gam.T
        glast = jnp.where(first_rows[:, :n], gam[last[0]:last[0] + 1, :], gam[last[1]:last[1] + 1, :])
        eg_all = jnp.exp(gam)
        kdf_all = jnp.exp(glast - gam)
        egl_all = jnp.exp(glast)
        for p in range(HEADS // 2):
            h0 = 2 * p
            cb = d * HEADS + h0
            cg = 2 * HEADS + d * HEADS + h0
            cols = lambda mat, base: lane_pair(mat[:, base:base + 1], mat[:, base + 1:base + 2])
            beta = cols(gbv, cb)
            eg = cols(eg_all, cg)
            grow = lane_pair(gam_t[cg:cg + 1, :], gam_t[cg + 1:cg + 2, :])
            dec = jnp.where(incl, jnp.exp(jnp.where(incl, cols(gam, cg) - grow, 0.0)), 0.0)
            q = q_ref[0, :, h0 * GDN_DK:(h0 + 2) * GDN_DK]
            k = q_ref[0, :, GDN_QK + h0 * GDN_DK:GDN_QK + (h0 + 2) * GDN_DK]
            v = q_ref[0, :, 2 * GDN_QK + h0 * GDN_DV:2 * GDN_QK + (h0 + 2) * GDN_DV]
            kb = k * beta
            kdf = k * cols(kdf_all, cg)
            chains.append(dict(
                incl=incl, strict=strict, order=order, dec=dec, q=q, k=k, kb=kb, vb=v * beta, kbeg=kb * eg,
                qd=q * eg, kd_t=jnp.concatenate([kdf[:, :n].T, kdf[:, n:].T], axis=1),
                gl=[lane_pair(egl_all[row:row + 1, cg:cg + 1], egl_all[row:row + 1, cg + 1:cg + 2])
                    for row in last]))

    for ch in chains:
        kk_qk = _dot_nt(jnp.concatenate([ch["kb"], ch["q"]], axis=0).astype(BF16), _pair_bd(ch["k"]))
        ch["a"] = jnp.where(ch["strict"], kk_qk[:n] * ch["dec"], 0.0)
        ch["qkm"] = jnp.where(ch["incl"], kk_qk[n:] * ch["dec"], 0.0)

    dmat = [jnp.where(blk16, ch["a"], 0.0) for ch in chains]
    nmat = [ch["a"] - dm for ch, dm in zip(chains, dmat)]
    dinv = [eye - dm for dm in dmat]
    dk = dmat
    for _ in range(3):
        dk = [_pmm(x, x) for x in dk]
        dinv = [di + _pmm(di, x) for di, x in zip(dinv, dk)]
    mmat = [_pmm(di, nn) for di, nn in zip(dinv, nmat)]
    m2 = [_pmm(mm, mm) for mm in mmat]
    imm = [eye - mm for mm in mmat]
    minv = [im + _pmm(im, x) for im, x in zip(imm, m2)]
    tmat = [_pmm(mi, di) for mi, di in zip(minv, dinv)]
    umat = [_pmm(t, ch["vb"]) for t, ch in zip(tmat, chains)]
    wmat = [_pmm(t, ch["kbeg"]) for t, ch in zip(tmat, chains)]

    state = [s_ref[i] for i in range(len(chains))]
    vnew = [[None, None] for _ in chains]
    qsum = [[None, None] for _ in chains]
    for pos in range(2):
        for i, ch in enumerate(chains):
            ci = ch["order"][pos]
            rows = slice(ci * c, (ci + 1) * c)
            wsqs = _pmm(jnp.concatenate([wmat[i][rows], ch["qd"][rows]], axis=0), state[i])
            vn = umat[i][rows] - wsqs[:c]
            vnew[i][ci] = vn
            qsum[i][ci] = wsqs[c:]
            vfull = jnp.concatenate([vn, zero_half] if ci == 0 else [zero_half, vn], axis=0)
            kd_c = jnp.where(first_cols if ci == 0 else jnp.logical_not(first_cols), ch["kd_t"], 0.0)
            state[i] = state[i] * ch["gl"][ci] + _pmm(kd_c, vfull)
    outs = [jnp.concatenate(qsum[i], axis=0) + _pmm(ch["qkm"], jnp.concatenate(vnew[i], axis=0))
            for i, ch in enumerate(chains)]
    half = len(chains) // 2
    of_ref[0] = jnp.concatenate(outs[:half], axis=1)
    ob_ref[0] = jnp.concatenate(outs[half:], axis=1)
    s_ref[...] = jnp.stack(state, axis=0)


def _gdn_scan(qkv_n, gb, n_ctx_steps):
    bsz, t_all, ch = qkv_n.shape
    n = GDN_STEP
    nsteps = t_all // n

    def bwd(i):
        return jnp.where(i < n_ctx_steps, n_ctx_steps - 1 - i, nsteps - 1 - (i - n_ctx_steps))

    out = jax.ShapeDtypeStruct((bsz, t_all, GDN_WIDTH), F32)
    return pl.pallas_call(
        _gscan_kernel,
        grid=(bsz, nsteps),
        in_specs=[
            pl.BlockSpec((1, n, ch), lambda b, i: (b, i, 0)),
            pl.BlockSpec((1, n, LANES), lambda b, i: (b, i, 0)),
            pl.BlockSpec((1, n, ch), lambda b, i: (b, bwd(i), 0)),
            pl.BlockSpec((1, n, LANES), lambda b, i: (b, bwd(i), 0)),
        ],
        out_specs=[
            pl.BlockSpec((1, n, GDN_WIDTH), lambda b, i: (b, i, 0)),
            pl.BlockSpec((1, n, GDN_WIDTH), lambda b, i: (b, bwd(i), 0)),
        ],
        out_shape=[out, out],
        scratch_shapes=[pltpu.VMEM((HEADS, GDN_DK, 2 * GDN_DV), F32)],
        compiler_params=pltpu.CompilerParams(
            dimension_semantics=("parallel", "arbitrary"), vmem_limit_bytes=VMEM_LIMIT),
        name="gdn_scan",
    )(qkv_n, gb, qkv_n, gb)


def _attn_kernel(n_maps, n_ctx_blocks, group, lam_init, q_ref, k_ref, vt_ref, z_ref, gain_ref, lam_ref,
                 o_ref, m_ref, acc_ref, s0_ref):
    i = pl.program_id(1)
    tq = q_ref.shape[1]
    nblk = vt_ref.shape[1]
    dva = vt_ref.shape[2] // HEADS
    dv = dva - VT_PAD
    n_groups = nblk // group

    m_ref[...] = jnp.full_like(m_ref, -1e30)
    acc_ref[...] = jnp.zeros_like(acc_ref)

    lane = lax.broadcasted_iota(jnp.int32, (tq, HEAD_PAD), 1)
    qs = []
    for hh in range(HEADS):
        qh = q_ref[0, :, hh * HEAD_PAD:(hh + 1) * HEAD_PAD]
        for mm in range(n_maps):
            if n_maps == 1:
                qs.append(qh)
            else:
                keep = jnp.logical_and(lane >= mm * DIFF_QK, lane < (mm + 1) * DIFF_QK)
                qs.append(jnp.where(keep, qh, jnp.zeros_like(qh)))

    nhm = HEADS * n_maps

    def value_matmul(hh, first_blk, nb, pb):
        return functools.reduce(jnp.add, [
            _dot(vt_ref[0, first_blk + b, hh * dva:(hh + 1) * dva, :], pb[b * KV_BLOCK:(b + 1) * KV_BLOCK])
            for b in range(nb)])

    def attend(first_blk, nb):
        rows = nb * KV_BLOCK
        start = pl.multiple_of(first_blk * KV_BLOCK, KV_BLOCK)
        for idx in range(nhm):
            hh = idx // n_maps
            kk = k_ref[0, pl.ds(start, rows), hh * HEAD_PAD:(hh + 1) * HEAD_PAD]
            s0_ref[idx, 0:rows, :] = _dot_nt(kk, qs[idx])
        m_out, acc_out = [], []
        for idx in range(nhm):
            hh = idx // n_maps
            m_prev = m_ref[idx]
            m_new = jnp.maximum(m_prev, jnp.max(s0_ref[idx, 0:rows, :], axis=0, keepdims=True))
            alpha = jnp.exp2(m_prev - m_new)
            pb = jnp.exp2(s0_ref[idx, 0:rows, :] - m_new).astype(BF16)
            acc_out.append(alpha * acc_ref[idx] + value_matmul(hh, first_blk, nb, pb))
            m_out.append(m_new)
        m_ref[...] = jnp.stack(m_out, axis=0)
        acc_ref[...] = jnp.stack(acc_out, axis=0)

    @pl.when(i < n_ctx_blocks)
    def _():
        attend(0, n_ctx_blocks)

    @pl.when(i >= n_ctx_blocks)
    def _():
        def body(g, carry):
            attend(g * group, group)
            return carry

        lax.fori_loop(0, n_groups, body, 0)

    outs = []
    for hh in range(HEADS):
        if n_maps == 1:
            o_t = acc_ref[hh, 0:dv] / acc_ref[hh, dv:dv + 1]
        else:
            lam = lam_ref[...]
            lq = jnp.sum(lam[0:1] * lam[1:2], axis=-1, keepdims=True)
            lk = jnp.sum(lam[2:3] * lam[3:4], axis=-1, keepdims=True)
            lam_full = jnp.exp(lq) - jnp.exp(lk) + lam_init
            o_t = (acc_ref[2 * hh, 0:dv] / acc_ref[2 * hh, dv:dv + 1]
                   - lam_full * (acc_ref[2 * hh + 1, 0:dv] / acc_ref[2 * hh + 1, dv:dv + 1]))
            ms = jnp.mean(o_t * o_t, axis=0, keepdims=True)
            o_t = o_t * lax.rsqrt(ms + NORM_EPS) * gain_ref[...] * (1.0 - lam_init)
        outs.append(o_t)
    o = jnp.concatenate(outs, axis=0).T
    o_ref[0] = o * _silu(z_ref[0])


def _attention(n_maps, lam_init, q, k, vt, z, gain_col, lam, n_ctx_blocks):
    bsz, t_all, width = q.shape
    tq = ROW_TILE
    nq = t_all // tq
    nblk = vt.shape[1]
    hv = z.shape[2]
    hva = vt.shape[2]
    nhm = HEADS * n_maps
    group = next(g for g in (3, 2, 1) if nblk % g == 0)
    stage_rows = max(group, n_ctx_blocks) * KV_BLOCK
    return pl.pallas_call(
        functools.partial(_attn_kernel, n_maps, n_ctx_blocks, group, lam_init),
        grid=(bsz, nq),
        in_specs=[
            pl.BlockSpec((1, tq, width), lambda b, i: (b, i, 0)),
            pl.BlockSpec((1, t_all, width), lambda b, i: (b, 0, 0)),
            pl.BlockSpec((1, nblk, hva, KV_BLOCK), lambda b, i: (b, 0, 0, 0)),
            pl.BlockSpec((1, tq, hv), lambda b, i: (b, i, 0)),
            pl.BlockSpec(gain_col.shape, lambda b, i: (0, 0)),
            pl.BlockSpec(lam.shape, lambda b, i: (0, 0)),
        ],
        out_specs=pl.BlockSpec((1, tq, hv), lambda b, i: (b, i, 0)),
        out_shape=jax.ShapeDtypeStruct((bsz, t_all, hv), F32),
        scratch_shapes=[pltpu.VMEM((nhm, 1, tq), F32),
                        pltpu.VMEM((nhm, hva // HEADS, tq), F32),
                        pltpu.VMEM((nhm, stage_rows, tq), F32)],
        compiler_params=pltpu.CompilerParams(
            dimension_semantics=("parallel", "arbitrary"), vmem_limit_bytes=VMEM_LIMIT),
        name="mla_attn" if n_maps == 1 else "diff_attn",
    )(q, k, vt, z, gain_col, lam)


def _out_kernel(x_ref, mod_ref, of_ref, ob_ref, za_ref, ag_ref, b_ref, c_ref, w_ref, pg_ref, o_ref):
    o_a = of_ref[0] + ob_ref[0]
    za = za_ref[0]
    parts = []
    for hh in range(HEADS):
        sl = slice(hh * GDN_DV, (hh + 1) * GDN_DV)
        parts.append(_rms(o_a[:, sl], ag_ref[...]) * _silu(za[:, sl]))
    parts.append(b_ref[0])
    parts.append(c_ref[0])
    cat = jnp.concatenate(parts, axis=1).astype(BF16)
    y = _dot(cat, w_ref[...])
    gate = mod_ref[0, 0][2:3]
    o_ref[0] = x_ref[0] + gate * _rms(y, pg_ref[...])


def _out_proj(xa, mod_l, o_f, o_b, z_a, a_gain, out_b, out_c, w_out, post_gain, n_ctx_tiles):
    bsz, t_all, _ = xa.shape
    tm = ROW_TILE
    nt = t_all // tm
    row = lambda width: pl.BlockSpec((1, tm, width), lambda b, i: (b, i, 0))
    full = lambda shape: pl.BlockSpec(shape, lambda b, i: (0,) * len(shape))
    return pl.pallas_call(
        _out_kernel,
        grid=(bsz, nt),
        in_specs=[
            row(D_MODEL),
            pl.BlockSpec((1, 1, 3, D_MODEL), lambda b, i: (b, jnp.where(i < n_ctx_tiles, 0, 1), 0, 0)),
            row(GDN_WIDTH), row(GDN_WIDTH), row(GDN_WIDTH), full((1, GDN_DV)),
            row(MLA_WIDTH), row(DIFF_WIDTH), full((D_MODEL, D_MODEL)), full((1, D_MODEL)),
        ],
        out_specs=row(D_MODEL),
        out_shape=jax.ShapeDtypeStruct(xa.shape, F32),
        compiler_params=pltpu.CompilerParams(
            dimension_semantics=("parallel", "arbitrary"), vmem_limit_bytes=VMEM_LIMIT),
        name="out_proj",
    )(xa, mod_l, o_f, o_b, z_a, a_gain, out_b, out_c, w_out, post_gain)


def _swap_halves(w):
    half = w.shape[-1] // 2
    return jnp.concatenate([w[..., half:], w[..., :half]], axis=-1)


def _pack_w_in(w):
    d = w.shape[0]
    z = lambda n: jnp.zeros((d, n), w.dtype)
    kr = w[:, _O_KR:_O_KR + MLA_ROPE]
    pad_r = HEAD_PAD - MLA_NOPE - MLA_ROPE
    kr_group = jnp.concatenate([z(MLA_NOPE), kr, z(pad_r), z(MLA_NOPE), _swap_halves(kr), z(pad_r)], axis=1)

    def diff_group(off):
        plain, swapped = [], []
        for hh in range(HEADS):
            m0 = w[:, off + hh * 2 * DIFF_QK:off + hh * 2 * DIFF_QK + DIFF_QK]
            m1 = w[:, off + hh * 2 * DIFF_QK + DIFF_QK:off + (hh + 1) * 2 * DIFF_QK]
            plain += [m0, m1, z(HEAD_PAD - 2 * DIFF_QK)]
            swapped += [_swap_halves(m0), _swap_halves(m1), z(HEAD_PAD - 2 * DIFF_QK)]
        return jnp.concatenate(plain + swapped, axis=1)

    groups = [
        w[:, _O_QKV:_O_QKV + GDN_CONV_CH],
        w[:, _O_ZA:_O_ZA + GDN_WIDTH],
        jnp.concatenate([w[:, _O_BL:_O_BL + 4 * HEADS], z(LANES - 4 * HEADS)], axis=1),
        w[:, _O_CQ:_O_CQ + MLA_Q_RANK],
        w[:, _O_CKV:_O_CKV + MLA_KV_RANK],
        kr_group,
        w[:, _O_ZB:_O_ZB + MLA_WIDTH],
        diff_group(_O_DQ),
        diff_group(_O_DK),
        w[:, _O_DV:_O_DV + DIFF_WIDTH],
        w[:, _O_DZ:_O_DZ + DIFF_WIDTH],
    ]
    return jnp.concatenate(groups, axis=1).astype(BF16)


def _pack_w_uq(w):
    d = w.shape[0]
    z = lambda n: jnp.zeros((d, n), w.dtype)
    per = MLA_NOPE + MLA_ROPE
    pad_r = HEAD_PAD - per
    plain, swapped = [], []
    for hh in range(HEADS):
        nope = w[:, hh * per:hh * per + MLA_NOPE]
        rope = w[:, hh * per + MLA_NOPE:(hh + 1) * per]
        plain += [nope, rope, z(pad_r)]
        swapped += [z(MLA_NOPE), _swap_halves(rope), z(pad_r)]
    return jnp.concatenate(plain + swapped, axis=1).astype(BF16)


def _pack_w_ukv(w):
    d = w.shape[0]
    z = lambda n: jnp.zeros((d, n), w.dtype)
    per = MLA_NOPE + MLA_V
    keys, vals = [], []
    for hh in range(HEADS):
        keys += [w[:, hh * per:hh * per + MLA_NOPE], z(HEAD_PAD - MLA_NOPE)]
        vals.append(w[:, hh * per + MLA_NOPE:(hh + 1) * per])
    return jnp.concatenate(keys + vals, axis=1).astype(BF16)


def _rope_tables(seq, ctx):
    n_rows = seq // GRID_W
    row = jnp.repeat(jnp.arange(n_rows), GRID_W).astype(F32)
    colp = jnp.tile(jnp.arange(GRID_W), n_rows).astype(F32)
    d_axis = MLA_ROPE // 2
    inv = 1.0 / (ROPE_BASE ** (jnp.arange(0, d_axis, 2, dtype=F32) / d_axis))
    ang = jnp.concatenate([row[:, None] * inv, colp[:, None] * inv], axis=-1)
    cos = jnp.concatenate([jnp.ones((ctx, ang.shape[1]), F32), jnp.cos(ang)], axis=0)
    sin = jnp.concatenate([jnp.zeros((ctx, ang.shape[1]), F32), jnp.sin(ang)], axis=0)
    t_all = seq + ctx
    ones = lambda n: jnp.ones((t_all, n), F32)
    zeros = lambda n: jnp.zeros((t_all, n), F32)
    pad_r = HEAD_PAD - MLA_NOPE - MLA_ROPE
    cos_m = jnp.concatenate([ones(MLA_NOPE), cos, cos, ones(pad_r)], axis=1)
    sin_m = jnp.concatenate([zeros(MLA_NOPE), -sin, sin, zeros(pad_r)], axis=1)
    pad_d = HEAD_PAD - 2 * DIFF_QK
    cos_d = jnp.concatenate([cos, cos, cos, cos, ones(pad_d)], axis=1)
    sin_d = jnp.concatenate([-sin, sin, -sin, sin, zeros(pad_d)], axis=1)
    return cos_m, sin_m, cos_d, sin_d


def _lane_row(p):
    flat = p.reshape(1, 2 * HEADS).astype(F32)
    return jnp.concatenate([jnp.zeros((1, 2 * HEADS), F32), flat,
                            jnp.zeros((1, LANES - 4 * HEADS), F32)], axis=1)


def kernel(x, c, ctx, c_ctx, ada_w, ada_b, pre_gain, post_gain, w_in, gdn_conv, gdn_a_log, gdn_dt_bias,
           gdn_out_gain, mla_q_gain, mla_w_uq, mla_kv_gain, mla_w_ukv, diff_lambda, diff_sub_gain, w_out):
    bsz, seq, _ = x.shape
    n_ctx = ctx.shape[1]
    depth = ada_w.shape[0]
    assert n_ctx % ROW_TILE == 0 and seq % ROW_TILE == 0 and seq % GRID_W == 0 and ROW_TILE == KV_BLOCK
    assert bsz + 1 <= SUBLANES
    n_ctx_tiles = n_ctx // ROW_TILE

    xa = jnp.concatenate([ctx, x], axis=1)
    cc = jnp.concatenate([c, c_ctx[None, :], jnp.zeros((SUBLANES - bsz - 1, D_MODEL), F32)], axis=0)
    mod = _modulation(cc, ada_w, ada_b)
    tabs = _rope_tables(seq, n_ctx)

    for l in range(depth):
        lam_init = 0.8 - 0.6 * math.exp(-0.3 * l)
        mod_rows = mod[l].reshape(SUBLANES, 3, D_MODEL)
        mod_l = jnp.stack([jnp.broadcast_to(mod_rows[bsz], (bsz, 3, D_MODEL)), mod_rows[:bsz]], axis=1)
        (p_qkv, z_a, gb, q_m, k_m, vt_m, z_b, q_d, k_d, vt_d, z_d) = _in_proj(
            xa, mod_l, pre_gain[l][None, :], _pack_w_in(w_in[l]),
            _lane_row(gdn_a_log[l]), _lane_row(gdn_dt_bias[l]),
            mla_q_gain[l][None, :], _pack_w_uq(mla_w_uq[l]),
            mla_kv_gain[l][None, :], _pack_w_ukv(mla_w_ukv[l]), tabs, n_ctx_tiles)
        qkv_n = _gdn_prepare(p_qkv, gdn_conv[l], n_ctx_tiles)
        o_f, o_b = _gdn_scan(qkv_n, gb, n_ctx // GDN_STEP)
        gain_col = diff_sub_gain[l][:, None]
        out_b = _attention(1, lam_init, q_m, k_m, vt_m, z_b, gain_col, diff_lambda[l], n_ctx // KV_BLOCK)
        out_c = _attention(2, lam_init, q_d, k_d, vt_d, z_d, gain_col, diff_lambda[l], n_ctx // KV_BLOCK)
        xa = _out_proj(xa, mod_l, o_f, o_b, z_a, gdn_out_gain[l][None, :], out_b, out_c,
                       w_out[l].astype(BF16), post_gain[l][None, :], n_ctx_tiles)
    return xa[:, n_ctx:]
```

```python
import functools
import math

import jax
import jax.numpy as jnp
import numpy as np
from jax import lax
from jax.experimental import pallas as pl
from jax.experimental.pallas import tpu as pltpu

F32 = jnp.float32
BF16 = jnp.bfloat16

D_MODEL = 1024
GRID_W = 64
NORM_EPS = 1e-6
ROPE_BASE = 10000.0

HEADS = 4
GDN_DK = 128
GDN_DV = 128
GDN_CHUNK = 64
CONV_W = 5
GDN_WIDTH = HEADS * GDN_DV
GDN_QK = HEADS * GDN_DK
GDN_CONV_CH = 2 * GDN_QK + GDN_WIDTH

MLA_Q_RANK = 256
MLA_KV_RANK = 128
MLA_NOPE = 64
MLA_ROPE = 32
MLA_V = 64
MLA_WIDTH = HEADS * MLA_V
MLA_SCALE = (MLA_NOPE + MLA_ROPE) ** -0.5

DIFF_QK = 32
DIFF_V = 64
DIFF_WIDTH = HEADS * DIFF_V
DIFF_QK_COLS = HEADS * 2 * DIFF_QK
DIFF_SCALE = DIFF_QK ** -0.5

LOG2E = 1.4426950408889634

LANES = 128
SUBLANES = 8
VMEM_LIMIT = 56 * 1024 * 1024

ROW_TILE = 256
KV_BLOCK = 256
GDN_STEP = 2 * GDN_CHUNK
HEAD_PAD = LANES
VT_PAD = 16
VT_ROWS = HEADS * (MLA_V + VT_PAD)

_O_QKV = 0
_O_ZA = _O_QKV + GDN_CONV_CH
_O_BL = _O_ZA + GDN_WIDTH
_O_AL = _O_BL + 2 * HEADS
_O_CQ = _O_AL + 2 * HEADS
_O_CKV = _O_CQ + MLA_Q_RANK
_O_KR = _O_CKV + MLA_KV_RANK
_O_ZB = _O_KR + MLA_ROPE
_O_DQ = _O_ZB + MLA_WIDTH
_O_DK = _O_DQ + DIFF_QK_COLS
_O_DV = _O_DK + DIFF_QK_COLS
_O_DZ = _O_DV + DIFF_WIDTH

_PACK = {}
_off = 0
for _name, _w in (("qkv", GDN_CONV_CH), ("za", GDN_WIDTH), ("ba", LANES), ("cq", MLA_Q_RANK),
                  ("ckv", MLA_KV_RANK), ("kr", 2 * LANES), ("zb", MLA_WIDTH),
                  ("dq", 2 * HEADS * HEAD_PAD), ("dk", 2 * HEADS * HEAD_PAD),
                  ("dv", DIFF_WIDTH), ("dz", DIFF_WIDTH)):
    _PACK[_name] = (_off, _w)
    _off += _w
PACK_COLS = _off


def _silu(x):
    return x * jax.nn.sigmoid(x)


def _rms(x, gain):
    return x * lax.rsqrt(jnp.mean(x * x, axis=-1, keepdims=True) + NORM_EPS) * gain


def _dot(a, b):
    return jnp.dot(a, b, preferred_element_type=F32)


def _dot_nt(a, b):
    return lax.dot_general(a, b, (((1,), (1,)), ((), ())), preferred_element_type=F32)


def _mod_kernel(c_ref, w_ref, b_ref, o_ref):
    a = _silu(c_ref[...])
    o_ref[0] = jnp.dot(a, w_ref[0], preferred_element_type=F32,
                       precision=lax.Precision.HIGHEST) + b_ref[0]


def _modulation(cc, ada_w, ada_b):
    depth = ada_w.shape[0]
    ncol = ada_w.shape[2]
    blk = D_MODEL
    return pl.pallas_call(
        _mod_kernel,
        grid=(depth, ncol // blk),
        in_specs=[
            pl.BlockSpec((SUBLANES, D_MODEL), lambda l, j: (0, 0)),
            pl.BlockSpec((1, D_MODEL, blk), lambda l, j: (l, 0, j)),
            pl.BlockSpec((1, 1, blk), lambda l, j: (l, 0, j)),
        ],
        out_specs=pl.BlockSpec((1, SUBLANES, blk), lambda l, j: (l, 0, j)),
        out_shape=jax.ShapeDtypeStruct((depth, SUBLANES, ncol), F32),
        compiler_params=pltpu.CompilerParams(
            dimension_semantics=("arbitrary", "arbitrary"), vmem_limit_bytes=VMEM_LIMIT),
        name="adaln_mod",
    )(cc, ada_w, ada_b.reshape(depth, 1, ncol))


def _augmented_vt(v):
    rows = v.shape[0]
    dv = v.shape[1] // HEADS
    vt = v.T
    ones_blk = jnp.where(lax.broadcasted_iota(jnp.int32, (VT_PAD, rows), 0) == 0, 1.0, 0.0).astype(F32)
    parts = []
    for hh in range(HEADS):
        parts += [vt[hh * dv:(hh + 1) * dv], ones_blk]
    return jnp.concatenate(parts, axis=0).astype(BF16)


def _in_kernel(n_ctx_tiles, xc_ref, xl_ref, mod_ref, pg_ref, w_ref, alog_ref, dtb_ref, qg_ref, wuq_ref,
               kvg_ref, wukv_ref, cm_ref, sm_ref, cd_ref, sd_ref,
               qkv_ref, za_ref, gb_ref, qm_ref, km_ref, vtm_ref, zb_ref,
               qd_ref, kd_ref, vtd_ref, zd_ref):
    x = jnp.where(pl.program_id(1) < n_ctx_tiles, xc_ref[0], xl_ref[0])
    mod = mod_ref[0, 0]
    shift = mod[0:1]
    scale = mod[1:2]
    h = (_rms(x, pg_ref[...]) * (1.0 + scale) + shift).astype(BF16)

    def proj(name):
        off, width = _PACK[name]
        return _dot(h, w_ref[:, off:off + width])

    qkv_ref[0] = proj("qkv")
    za_ref[0] = proj("za")
    ba = proj("ba")
    lane = lax.broadcasted_iota(jnp.int32, ba.shape, 1)
    beta = jax.nn.sigmoid(ba)
    zz = ba + dtb_ref[...]
    softplus = jnp.maximum(zz, 0.0) + jnp.log(1.0 + jnp.exp(-jnp.abs(zz)))
    gdec = -jnp.exp(alog_ref[...]) * softplus
    gb_ref[0] = jnp.where(lane < 2 * HEADS, beta, jnp.where(lane < 4 * HEADS, gdec, 0.0))

    cm = cm_ref[...]
    sm = sm_ref[...]
    nq = _rms(proj("cq"), qg_ref[...]).astype(BF16)
    qq = _dot(nq, wuq_ref[...])
    nkv = _rms(proj("ckv"), kvg_ref[...]).astype(BF16)
    kvv = _dot(nkv, wukv_ref[...])
    kr = proj("kr")
    krr = kr[:, :LANES] * cm + kr[:, LANES:] * sm
    half = HEADS * HEAD_PAD
    for hh in range(HEADS):
        sl = slice(hh * HEAD_PAD, (hh + 1) * HEAD_PAD)
        sw = slice(half + hh * HEAD_PAD, half + (hh + 1) * HEAD_PAD)
        qm_ref[0, :, sl] = ((qq[:, sl] * cm + qq[:, sw] * sm) * (MLA_SCALE * LOG2E)).astype(BF16)
        km_ref[0, :, sl] = (kvv[:, sl] + krr).astype(BF16)
    vtm_ref[0, 0] = _augmented_vt(kvv[:, half:half + MLA_WIDTH])
    zb_ref[0] = proj("zb")

    cd = cd_ref[...]
    sd = sd_ref[...]
    dq = proj("dq")
    dk = proj("dk")
    for hh in range(HEADS):
        sl = slice(hh * HEAD_PAD, (hh + 1) * HEAD_PAD)
        sw = slice(half + hh * HEAD_PAD, half + (hh + 1) * HEAD_PAD)
        qd_ref[0, :, sl] = ((dq[:, sl] * cd + dq[:, sw] * sd) * (DIFF_SCALE * LOG2E)).astype(BF16)
        kd_ref[0, :, sl] = (dk[:, sl] * cd + dk[:, sw] * sd).astype(BF16)
    vtd_ref[0, 0] = _augmented_vt(proj("dv"))
    zd_ref[0] = proj("dz")


def _token_specs(tm, n_ctx_tiles):
    ctx = pl.BlockSpec((1, tm, D_MODEL), lambda b, i: (b, jnp.minimum(i, n_ctx_tiles - 1), 0))
    lat = pl.BlockSpec((1, tm, D_MODEL), lambda b, i: (b, jnp.maximum(i - n_ctx_tiles, 0), 0))
    return ctx, lat


def _in_proj(xc, xl, mod_l, pre_gain, w_pack, alog_row, dtb_row, q_gain, wuq, kv_gain, wukv, tabs, n_ctx_tiles):
    bsz = xl.shape[0]
    t_all = xc.shape[1] + xl.shape[1]
    tm = ROW_TILE
    nt = t_all // tm
    full = lambda shape: pl.BlockSpec(shape, lambda b, i: (0,) * len(shape))
    row = lambda width: pl.BlockSpec((1, tm, width), lambda b, i: (b, i, 0))
    tab = pl.BlockSpec((tm, LANES), lambda b, i: (i, 0))
    vt = pl.BlockSpec((1, 1, VT_ROWS, tm), lambda b, i: (b, i, 0, 0))
    act = lambda width, dt: jax.ShapeDtypeStruct((bsz, t_all, width), dt)
    vts = jax.ShapeDtypeStruct((bsz, nt, VT_ROWS, tm), BF16)
    return pl.pallas_call(
        functools.partial(_in_kernel, n_ctx_tiles),
        grid=(bsz, nt),
        in_specs=[
            *_token_specs(tm, n_ctx_tiles),
            pl.BlockSpec((1, 1, 3, D_MODEL), lambda b, i: (b, jnp.where(i < n_ctx_tiles, 0, 1), 0, 0)),
            full((1, D_MODEL)),
            full((D_MODEL, PACK_COLS)),
            full((1, LANES)), full((1, LANES)),
            full((1, MLA_Q_RANK)), full((MLA_Q_RANK, 2 * HEADS * HEAD_PAD)),
            full((1, MLA_KV_RANK)), full((MLA_KV_RANK, HEADS * HEAD_PAD + MLA_WIDTH)),
            tab, tab, tab, tab,
        ],
        out_specs=[row(GDN_CONV_CH), row(GDN_WIDTH), row(LANES),
                   row(HEADS * HEAD_PAD), row(HEADS * HEAD_PAD), vt, row(MLA_WIDTH),
                   row(HEADS * HEAD_PAD), row(HEADS * HEAD_PAD), vt, row(DIFF_WIDTH)],
        out_shape=[act(GDN_CONV_CH, F32), act(GDN_WIDTH, F32), act(LANES, F32),
                   act(HEADS * HEAD_PAD, BF16), act(HEADS * HEAD_PAD, BF16), vts, act(MLA_WIDTH, F32),
                   act(HEADS * HEAD_PAD, BF16), act(HEADS * HEAD_PAD, BF16), vts, act(DIFF_WIDTH, F32)],
        compiler_params=pltpu.CompilerParams(
            dimension_semantics=("parallel", "arbitrary"), vmem_limit_bytes=VMEM_LIMIT),
        name="in_proj",
    )(xc, xl, mod_l, pre_gain, w_pack, alog_row, dtb_row, q_gain, wuq, kv_gain, wukv, *tabs)


def _gprep_kernel(n_ctx_tiles, cur_ref, prev_ref, next_ref, w_ref, o_ref, ext_ref):
    i = pl.program_id(1)
    nt = pl.num_programs(1)
    tm = cur_ref.shape[1]
    halo = CONV_W // 2
    has_prev = jnp.logical_and(i != 0, i != n_ctx_tiles)
    has_next = jnp.logical_and(i != n_ctx_tiles - 1, i != nt - 1)
    ext_ref[0:SUBLANES] = jnp.where(has_prev, prev_ref[0], 0.0)
    ext_ref[SUBLANES:SUBLANES + tm] = cur_ref[0]
    ext_ref[SUBLANES + tm:2 * SUBLANES + tm] = jnp.where(has_next, next_ref[0], 0.0)
    acc = None
    for j in range(CONV_W):
        term = ext_ref[pl.ds(SUBLANES - halo + j, tm), :] * w_ref[j:j + 1, :]
        acc = term if acc is None else acc + term
    y = _silu(acc)
    for hh in range(2 * HEADS):
        sl = slice(hh * GDN_DK, (hh + 1) * GDN_DK)
        v = y[:, sl]
        v = v * lax.rsqrt(jnp.sum(v * v, axis=-1, keepdims=True) + NORM_EPS)
        if hh < HEADS:
            v = v * (GDN_DK ** -0.5)
        o_ref[0, :, sl] = v
    o_ref[0, :, 2 * GDN_QK:] = y[:, 2 * GDN_QK:]


def _gdn_prepare(p_qkv, conv_w, n_ctx_tiles):
    bsz, t_all, ch = p_qkv.shape
    tm = ROW_TILE
    nt = t_all // tm
    per = tm // SUBLANES
    last = t_all // SUBLANES - 1
    return pl.pallas_call(
        functools.partial(_gprep_kernel, n_ctx_tiles),
        grid=(bsz, nt),
        in_specs=[
            pl.BlockSpec((1, tm, ch), lambda b, i: (b, i, 0)),
            pl.BlockSpec((1, SUBLANES, ch), lambda b, i: (b, jnp.maximum(i * per - 1, 0), 0)),
            pl.BlockSpec((1, SUBLANES, ch), lambda b, i: (b, jnp.minimum((i + 1) * per, last), 0)),
            pl.BlockSpec((CONV_W, ch), lambda b, i: (0, 0)),
        ],
        out_specs=pl.BlockSpec((1, tm, ch), lambda b, i: (b, i, 0)),
        out_shape=jax.ShapeDtypeStruct((bsz, t_all, ch), F32),
        scratch_shapes=[pltpu.VMEM((tm + 2 * SUBLANES, ch), F32)],
        compiler_params=pltpu.CompilerParams(
            dimension_semantics=("parallel", "arbitrary"), vmem_limit_bytes=VMEM_LIMIT),
        name="gdn_prep",
    )(p_qkv, p_qkv, p_qkv, conv_w)


def _mm(a, b):
    return _dot(a.astype(BF16), b.astype(BF16))


def _pair_bd(x):
    n = x.shape[0]
    xb = x.astype(BF16)
    z = jnp.zeros((n, n), BF16)
    return jnp.concatenate([jnp.concatenate([xb[:, :n], z], axis=1),
                            jnp.concatenate([z, xb[:, n:]], axis=1)], axis=0)


def _pmm(a, b):
    return _dot(a.astype(BF16), _pair_bd(b))


def _gscan_kernel(qf_ref, gf_ref, qb_ref, gb_ref, of_ref, ob_ref, s_ref):
    @pl.when(pl.program_id(0) == 0)
    def _():
        s_ref[...] = jnp.zeros_like(s_ref)

    n = GDN_STEP
    c = GDN_CHUNK
    r = lax.broadcasted_iota(jnp.int32, (n, 2 * n), 0)
    col = lax.broadcasted_iota(jnp.int32, (n, 2 * n), 1) & (n - 1)
    same = (r >= c) == (col >= c)
    eye = jnp.where(r == col, 1.0, 0.0).astype(F32)
    blk16 = (r // 16) == (col // 16)
    first_rows = r < c
    first_cols = col < c
    zero_half = jnp.zeros((c, 2 * n), F32)

    def lane_pair(x0, x1):
        return jnp.concatenate([jnp.broadcast_to(x0, (n, n)), jnp.broadcast_to(x1, (n, n))], axis=1)

    bsz = qf_ref.shape[0]
    chains = []
    for bi in range(bsz):
        for d, (q_ref, g_ref) in enumerate(((qf_ref, gf_ref), (qb_ref, gb_ref))):
            if d == 0:
                incl = jnp.logical_and(same, r >= col)
                strict = jnp.logical_and(same, r > col)
                last = (c - 1, n - 1)
                order = (0, 1)
            else:
                incl = jnp.logical_and(same, r <= col)
                strict = jnp.logical_and(same, r < col)
                last = (0, c)
                order = (1, 0)
            gbv = g_ref[bi]
            gam = jnp.dot(jnp.where(incl[:, :n], 1.0, 0.0).astype(F32), gbv, preferred_element_type=F32,
                          precision=lax.Precision.HIGHEST)
            gam_t = gam.T
            glast = jnp.where(first_rows[:, :n], gam[last[0]:last[0] + 1, :], gam[last[1]:last[1] + 1, :])
            eg_all = jnp.exp(gam)
            kdf_all = jnp.exp(glast - gam)
            egl_all = jnp.exp(glast)
            for p in range(HEADS // 2):
                h0 = 2 * p
                cb = d * HEADS + h0
                cg = 2 * HEADS + d * HEADS + h0
                cols = lambda mat, base: lane_pair(mat[:, base:base + 1], mat[:, base + 1:base + 2])
                beta = cols(gbv, cb)
                eg = cols(eg_all, cg)
                grow = lane_pair(gam_t[cg:cg + 1, :], gam_t[cg + 1:cg + 2, :])
                dec = jnp.where(incl, jnp.exp(jnp.where(incl, cols(gam, cg) - grow, 0.0)), 0.0)
                q = q_ref[bi, :, h0 * GDN_DK:(h0 + 2) * GDN_DK]
                k = q_ref[bi, :, GDN_QK + h0 * GDN_DK:GDN_QK + (h0 + 2) * GDN_DK]
                v = q_ref[bi, :, 2 * GDN_QK + h0 * GDN_DV:2 * GDN_QK + (h0 + 2) * GDN_DV]
                kb = k * beta
                kdf = k * cols(kdf_all, cg)
                chains.append(dict(
                    incl=incl, strict=strict, order=order, dec=dec, q=q, k=k, kb=kb, vb=v * beta,
                    kbeg=kb * eg, qd=q * eg,
                    kd_t=jnp.concatenate([kdf[:, :n].T, kdf[:, n:].T], axis=1),
                    gl=[lane_pair(egl_all[row:row + 1, cg:cg + 1], egl_all[row:row + 1, cg + 1:cg + 2])
                        for row in last]))

    for ch in chains:
        kk_qk = _dot_nt(jnp.concatenate([ch["kb"], ch["q"]], axis=0).astype(BF16), _pair_bd(ch["k"]))
        ch["a"] = jnp.where(ch["strict"], kk_qk[:n] * ch["dec"], 0.0)
        ch["qkm"] = jnp.where(ch["incl"], kk_qk[n:] * ch["dec"], 0.0)

    dmat = [jnp.where(blk16, ch["a"], 0.0) for ch in chains]
    nmat = [ch["a"] - dm for ch, dm in zip(chains, dmat)]
    dinv = [eye - dm for dm in dmat]
    dk = dmat
    for _ in range(3):
        dk = [_pmm(x, x) for x in dk]
        dinv = [di + _pmm(di, x) for di, x in zip(dinv, dk)]
    mmat = [_pmm(di, nn) for di, nn in zip(dinv, nmat)]
    m2 = [_pmm(mm, mm) for mm in mmat]
    imm = [eye - mm for mm in mmat]
    minv = [im + _pmm(im, x) for im, x in zip(imm, m2)]
    tmat = [_pmm(mi, di) for mi, di in zip(minv, dinv)]
    umat = [_pmm(t, ch["vb"]) for t, ch in zip(tmat, chains)]
    wmat = [_pmm(t, ch["kbeg"]) for t, ch in zip(tmat, chains)]

    state = [s_ref[i] for i in range(len(chains))]
    vnew = [[None, None] for _ in chains]
    qsum = [[None, None] for _ in chains]
    for pos in range(2):
        for i, ch in enumerate(chains):
            ci = ch["order"][pos]
            rows = slice(ci * c, (ci + 1) * c)
            wsqs = _pmm(jnp.concatenate([wmat[i][rows], ch["qd"][rows]], axis=0), state[i])
            vn = umat[i][rows] - wsqs[:c]
            vnew[i][ci] = vn
            qsum[i][ci] = wsqs[c:]
            vfull = jnp.concatenate([vn, zero_half] if ci == 0 else [zero_half, vn], axis=0)
            kd_c = jnp.where(first_cols if ci == 0 else jnp.logical_not(first_cols), ch["kd_t"], 0.0)
            state[i] = state[i] * ch["gl"][ci] + _pmm(kd_c, vfull)
    outs = [jnp.concatenate(qsum[i], axis=0) + _pmm(ch["qkm"], jnp.concatenate(vnew[i], axis=0))
            for i, ch in enumerate(chains)]
    per_dir = HEADS // 2
    of_ref[...] = jnp.stack([jnp.concatenate(outs[2 * bi * per_dir:(2 * bi + 1) * per_dir], axis=1)
                             for bi in range(bsz)], axis=0)
    ob_ref[...] = jnp.stack([jnp.concatenate(outs[(2 * bi + 1) * per_dir:(2 * bi + 2) * per_dir], axis=1)
                             for bi in range(bsz)], axis=0)
    s_ref[...] = jnp.stack(state, axis=0)


def _gdn_scan(qkv_n, gb, n_ctx_steps):
    bsz, t_all, ch = qkv_n.shape
    n = GDN_STEP
    nsteps = t_all // n

    def bwd(i):
        return jnp.where(i < n_ctx_steps, n_ctx_steps - 1 - i, nsteps - 1 - (i - n_ctx_steps))

    out = jax.ShapeDtypeStruct((bsz, t_all, GDN_WIDTH), F32)
    return pl.pallas_call(
        _gscan_kernel,
        grid=(nsteps,),
        in_specs=[
            pl.BlockSpec((bsz, n, ch), lambda i: (0, i, 0)),
            pl.BlockSpec((bsz, n, LANES), lambda i: (0, i, 0)),
            pl.BlockSpec((bsz, n, ch), lambda i: (0, bwd(i), 0)),
            pl.BlockSpec((bsz, n, LANES), lambda i: (0, bwd(i), 0)),
        ],
        out_specs=[
            pl.BlockSpec((bsz, n, GDN_WIDTH), lambda i: (0, i, 0)),
            pl.BlockSpec((bsz, n, GDN_WIDTH), lambda i: (0, bwd(i), 0)),
        ],
        out_shape=[out, out],
        scratch_shapes=[pltpu.VMEM((bsz * HEADS, GDN_DK, 2 * GDN_DV), F32)],
        compiler_params=pltpu.CompilerParams(
            dimension_semantics=("arbitrary",), vmem_limit_bytes=VMEM_LIMIT),
        name="gdn_scan",
    )(qkv_n, gb, qkv_n, gb)


def _attn_kernel(n_maps, n_ctx_blocks, group, lam_init, q_ref, k_ref, vt_ref, z_ref, gain_ref, lam_ref,
                 o_ref, m_ref, acc_ref, s0_ref):
    i = pl.program_id(1)
    tq = q_ref.shape[1]
    nblk = vt_ref.shape[1]
    dva = vt_ref.shape[2] // HEADS
    dv = dva - VT_PAD
    n_groups = nblk // group

    m_ref[...] = jnp.full_like(m_ref, -1e30)
    acc_ref[...] = jnp.zeros_like(acc_ref)

    lane = lax.broadcasted_iota(jnp.int32, (tq, HEAD_PAD), 1)
    qs = []
    for hh in range(HEADS):
        qh = q_ref[0, :, hh * HEAD_PAD:(hh + 1) * HEAD_PAD]
        for mm in range(n_maps):
            if n_maps == 1:
                qs.append(qh)
            else:
                keep = jnp.logical_and(lane >= mm * DIFF_QK, lane < (mm + 1) * DIFF_QK)
                qs.append(jnp.where(keep, qh, jnp.zeros_like(qh)))

    nhm = HEADS * n_maps

    def value_matmul(hh, first_blk, nb, pb):
        return functools.reduce(jnp.add, [
            _dot(vt_ref[0, first_blk + b, hh * dva:(hh + 1) * dva, :], pb[b * KV_BLOCK:(b + 1) * KV_BLOCK])
            for b in range(nb)])

    def attend(first_blk, nb):
        rows = nb * KV_BLOCK
        start = pl.multiple_of(first_blk * KV_BLOCK, KV_BLOCK)
        for idx in range(nhm):
            hh = idx // n_maps
            kk = k_ref[0, pl.ds(start, rows), hh * HEAD_PAD:(hh + 1) * HEAD_PAD]
            s0_ref[idx, 0:rows, :] = _dot_nt(kk, qs[idx])
        m_out, acc_out = [], []
        for idx in range(nhm):
            hh = idx // n_maps
            m_prev = m_ref[idx]
            m_new = jnp.maximum(m_prev, jnp.max(s0_ref[idx, 0:rows, :], axis=0, keepdims=True))
            alpha = jnp.exp2(m_prev - m_new)
            pb = jnp.exp2(s0_ref[idx, 0:rows, :] - m_new).astype(BF16)
            acc_out.append(alpha * acc_ref[idx] + value_matmul(hh, first_blk, nb, pb))
            m_out.append(m_new)
        m_ref[...] = jnp.stack(m_out, axis=0)
        acc_ref[...] = jnp.stack(acc_out, axis=0)

    @pl.when(i < n_ctx_blocks)
    def _():
        attend(0, n_ctx_blocks)

    @pl.when(i >= n_ctx_blocks)
    def _():
        def body(g, carry):
            attend(g * group, group)
            return carry

        lax.fori_loop(0, n_groups, body, 0)

    outs = []
    for hh in range(HEADS):
        if n_maps == 1:
            o_t = acc_ref[hh, 0:dv] / acc_ref[hh, dv:dv + 1]
        else:
            lam = lam_ref[...]
            lq = jnp.sum(lam[0:1] * lam[1:2], axis=-1, keepdims=True)
            lk = jnp.sum(lam[2:3] * lam[3:4], axis=-1, keepdims=True)
            lam_full = jnp.exp(lq) - jnp.exp(lk) + lam_init
            o_t = (acc_ref[2 * hh, 0:dv] / acc_ref[2 * hh, dv:dv + 1]
                   - lam_full * (acc_ref[2 * hh + 1, 0:dv] / acc_ref[2 * hh + 1, dv:dv + 1]))
            ms = jnp.mean(o_t * o_t, axis=0, keepdims=True)
            o_t = o_t * lax.rsqrt(ms + NORM_EPS) * gain_ref[...] * (1.0 - lam_init)
        outs.append(o_t)
    o = jnp.concatenate(outs, axis=0).T
    o_ref[0] = o * _silu(z_ref[0])


def _attention(n_maps, lam_init, q, k, vt, z, gain_col, lam, n_ctx_blocks):
    bsz, t_all, width = q.shape
    tq = ROW_TILE
    nq = t_all // tq
    nblk = vt.shape[1]
    hv = z.shape[2]
    hva = vt.shape[2]
    nhm = HEADS * n_maps
    group = next(g for g in (11, 3, 2, 1) if nblk % g == 0)
    stage_rows = max(group, n_ctx_blocks) * KV_BLOCK
    return pl.pallas_call(
        functools.partial(_attn_kernel, n_maps, n_ctx_blocks, group, lam_init),
        grid=(bsz, nq),
        in_specs=[
            pl.BlockSpec((1, tq, width), lambda b, i: (b, i, 0)),
            pl.BlockSpec((1, t_all, width), lambda b, i: (b, 0, 0), pipeline_mode=pl.Buffered(1)),
            pl.BlockSpec((1, nblk, hva, KV_BLOCK), lambda b, i: (b, 0, 0, 0), pipeline_mode=pl.Buffered(1)),
            pl.BlockSpec((1, tq, hv), lambda b, i: (b, i, 0)),
            pl.BlockSpec(gain_col.shape, lambda b, i: (0, 0)),
            pl.BlockSpec(lam.shape, lambda b, i: (0, 0)),
        ],
        out_specs=pl.BlockSpec((1, tq, hv), lambda b, i: (b, i, 0)),
        out_shape=jax.ShapeDtypeStruct((bsz, t_all, hv), F32),
        scratch_shapes=[pltpu.VMEM((nhm, 1, tq), F32),
                        pltpu.VMEM((nhm, hva // HEADS, tq), F32),
                        pltpu.VMEM((nhm, stage_rows, tq), F32)],
        compiler_params=pltpu.CompilerParams(
            dimension_semantics=("parallel", "arbitrary"), vmem_limit_bytes=VMEM_LIMIT),
        name="mla_attn" if n_maps == 1 else "diff_attn",
    )(q, k, vt, z, gain_col, lam)


def _out_kernel(n_ctx_tiles, xc_ref, xl_ref, mod_ref, of_ref, ob_ref, za_ref, ag_ref, b_ref, c_ref, w_ref,
                pg_ref, oc_ref, ol_ref):
    i = pl.program_id(1)
    o_a = of_ref[0] + ob_ref[0]
    za = za_ref[0]
    parts = []
    for hh in range(HEADS):
        sl = slice(hh * GDN_DV, (hh + 1) * GDN_DV)
        parts.append(_rms(o_a[:, sl], ag_ref[...]) * _silu(za[:, sl]))
    parts.append(b_ref[0])
    parts.append(c_ref[0])
    cat = jnp.concatenate(parts, axis=1).astype(BF16)
    y = _dot(cat, w_ref[...])
    upd = mod_ref[0, 0][2:3] * _rms(y, pg_ref[...])

    @pl.when(i < n_ctx_tiles)
    def _():
        oc_ref[0] = xc_ref[0] + upd

    @pl.when(i >= n_ctx_tiles)
    def _():
        ol_ref[0] = xl_ref[0] + upd


def _out_proj(xc, xl, mod_l, o_f, o_b, z_a, a_gain, out_b, out_c, w_out, post_gain, n_ctx_tiles):
    bsz = xl.shape[0]
    t_all = xc.shape[1] + xl.shape[1]
    tm = ROW_TILE
    nt = t_all // tm
    row = lambda width: pl.BlockSpec((1, tm, width), lambda b, i: (b, i, 0))
    full = lambda shape: pl.BlockSpec(shape, lambda b, i: (0,) * len(shape))
    return pl.pallas_call(
        functools.partial(_out_kernel, n_ctx_tiles),
        grid=(bsz, nt),
        in_specs=[
            *_token_specs(tm, n_ctx_tiles),
            pl.BlockSpec((1, 1, 3, D_MODEL), lambda b, i: (b, jnp.where(i < n_ctx_tiles, 0, 1), 0, 0)),
            row(GDN_WIDTH), row(GDN_WIDTH), row(GDN_WIDTH), full((1, GDN_DV)),
            row(MLA_WIDTH), row(DIFF_WIDTH), full((D_MODEL, D_MODEL)), full((1, D_MODEL)),
        ],
        out_specs=list(_token_specs(tm, n_ctx_tiles)),
        out_shape=[jax.ShapeDtypeStruct(xc.shape, F32), jax.ShapeDtypeStruct(xl.shape, F32)],
        compiler_params=pltpu.CompilerParams(
            dimension_semantics=("parallel", "arbitrary"), vmem_limit_bytes=VMEM_LIMIT),
        name="out_proj",
    )(xc, xl, mod_l, o_f, o_b, z_a, a_gain, out_b, out_c, w_out, post_gain)


def _swap_halves(w):
    half = w.shape[-1] // 2
    return jnp.concatenate([w[..., half:], w[..., :half]], axis=-1)


def _pack_w_in(w):
    d = w.shape[0]
    z = lambda n: jnp.zeros((d, n), w.dtype)
    kr = w[:, _O_KR:_O_KR + MLA_ROPE]
    pad_r = HEAD_PAD - MLA_NOPE - MLA_ROPE
    kr_group = jnp.concatenate([z(MLA_NOPE), kr, z(pad_r), z(MLA_NOPE), _swap_halves(kr), z(pad_r)], axis=1)

    def diff_group(off):
        plain, swapped = [], []
        for hh in range(HEADS):
            m0 = w[:, off + hh * 2 * DIFF_QK:off + hh * 2 * DIFF_QK + DIFF_QK]
            m1 = w[:, off + hh * 2 * DIFF_QK + DIFF_QK:off + (hh + 1) * 2 * DIFF_QK]
            plain += [m0, m1, z(HEAD_PAD - 2 * DIFF_QK)]
            swapped += [_swap_halves(m0), _swap_halves(m1), z(HEAD_PAD - 2 * DIFF_QK)]
        return jnp.concatenate(plain + swapped, axis=1)

    groups = [
        w[:, _O_QKV:_O_QKV + GDN_CONV_CH],
        w[:, _O_ZA:_O_ZA + GDN_WIDTH],
        jnp.concatenate([w[:, _O_BL:_O_BL + 4 * HEADS], z(LANES - 4 * HEADS)], axis=1),
        w[:, _O_CQ:_O_CQ + MLA_Q_RANK],
        w[:, _O_CKV:_O_CKV + MLA_KV_RANK],
        kr_group,
        w[:, _O_ZB:_O_ZB + MLA_WIDTH],
        diff_group(_O_DQ),
        diff_group(_O_DK),
        w[:, _O_DV:_O_DV + DIFF_WIDTH],
        w[:, _O_DZ:_O_DZ + DIFF_WIDTH],
    ]
    return jnp.concatenate(groups, axis=1).astype(BF16)


def _pack_w_uq(w):
    d = w.shape[0]
    z = lambda n: jnp.zeros((d, n), w.dtype)
    per = MLA_NOPE + MLA_ROPE
    pad_r = HEAD_PAD - per
    plain, swapped = [], []
    for hh in range(HEADS):
        nope = w[:, hh * per:hh * per + MLA_NOPE]
        rope = w[:, hh * per + MLA_NOPE:(hh + 1) * per]
        plain += [nope, rope, z(pad_r)]
        swapped += [z(MLA_NOPE), _swap_halves(rope), z(pad_r)]
    return jnp.concatenate(plain + swapped, axis=1).astype(BF16)


def _pack_w_ukv(w):
    d = w.shape[0]
    z = lambda n: jnp.zeros((d, n), w.dtype)
    per = MLA_NOPE + MLA_V
    keys, vals = [], []
    for hh in range(HEADS):
        keys += [w[:, hh * per:hh * per + MLA_NOPE], z(HEAD_PAD - MLA_NOPE)]
        vals.append(w[:, hh * per + MLA_NOPE:(hh + 1) * per])
    return jnp.concatenate(keys + vals, axis=1).astype(BF16)


def _rope_tables(seq, ctx):
    n_rows = seq // GRID_W
    row = np.repeat(np.arange(n_rows), GRID_W).astype(np.float64)
    colp = np.tile(np.arange(GRID_W), n_rows).astype(np.float64)
    d_axis = MLA_ROPE // 2
    inv = 1.0 / (ROPE_BASE ** (np.arange(0, d_axis, 2, dtype=np.float64) / d_axis))
    ang = np.concatenate([row[:, None] * inv, colp[:, None] * inv], axis=-1)
    cos = np.concatenate([np.ones((ctx, ang.shape[1])), np.cos(ang)], axis=0)
    sin = np.concatenate([np.zeros((ctx, ang.shape[1])), np.sin(ang)], axis=0)
    t_all = seq + ctx
    ones = lambda n: np.ones((t_all, n))
    zeros = lambda n: np.zeros((t_all, n))
    pad_r = HEAD_PAD - MLA_NOPE - MLA_ROPE
    cos_m = np.concatenate([ones(MLA_NOPE), cos, cos, ones(pad_r)], axis=1)
    sin_m = np.concatenate([zeros(MLA_NOPE), -sin, sin, zeros(pad_r)], axis=1)
    pad_d = HEAD_PAD - 2 * DIFF_QK
    cos_d = np.concatenate([cos, cos, cos, cos, ones(pad_d)], axis=1)
    sin_d = np.concatenate([-sin, sin, -sin, sin, zeros(pad_d)], axis=1)
    return tuple(jnp.asarray(t, dtype=F32) for t in (cos_m, sin_m, cos_d, sin_d))


def _lane_row(p):
    flat = p.reshape(1, 2 * HEADS).astype(F32)
    return jnp.concatenate([jnp.zeros((1, 2 * HEADS), F32), flat,
                            jnp.zeros((1, LANES - 4 * HEADS), F32)], axis=1)


def kernel(x, c, ctx, c_ctx, ada_w, ada_b, pre_gain, post_gain, w_in, gdn_conv, gdn_a_log, gdn_dt_bias,
           gdn_out_gain, mla_q_gain, mla_w_uq, mla_kv_gain, mla_w_ukv, diff_lambda, diff_sub_gain, w_out):
    bsz, seq, _ = x.shape
    n_ctx = ctx.shape[1]
    depth = ada_w.shape[0]
    assert n_ctx % ROW_TILE == 0 and seq % ROW_TILE == 0 and seq % GRID_W == 0 and ROW_TILE == KV_BLOCK
    assert bsz + 1 <= SUBLANES
    n_ctx_tiles = n_ctx // ROW_TILE

    cc = jnp.concatenate([c, c_ctx[None, :], jnp.zeros((SUBLANES - bsz - 1, D_MODEL), F32)], axis=0)
    mod = _modulation(cc, ada_w, ada_b)
    tabs = _rope_tables(seq, n_ctx)

    xc, xl = ctx, x
    for l in range(depth):
        lam_init = 0.8 - 0.6 * math.exp(-0.3 * l)
        mod_rows = mod[l].reshape(SUBLANES, 3, D_MODEL)
        mod_l = jnp.stack([jnp.broadcast_to(mod_rows[bsz], (bsz, 3, D_MODEL)), mod_rows[:bsz]], axis=1)
        (p_qkv, z_a, gb, q_m, k_m, vt_m, z_b, q_d, k_d, vt_d, z_d) = _in_proj(
            xc, xl, mod_l, pre_gain[l][None, :], _pack_w_in(w_in[l]),
            _lane_row(gdn_a_log[l]), _lane_row(gdn_dt_bias[l]),
            mla_q_gain[l][None, :], _pack_w_uq(mla_w_uq[l]),
            mla_kv_gain[l][None, :], _pack_w_ukv(mla_w_ukv[l]), tabs, n_ctx_tiles)
        qkv_n = _gdn_prepare(p_qkv, gdn_conv[l], n_ctx_tiles)
        o_f, o_b = _gdn_scan(qkv_n, gb, n_ctx // GDN_STEP)
        gain_col = diff_sub_gain[l][:, None]
        out_b = _attention(1, lam_init, q_m, k_m, vt_m, z_b, gain_col, diff_lambda[l], n_ctx // KV_BLOCK)
        out_c = _attention(2, lam_init, q_d, k_d, vt_d, z_d, gain_col, diff_lambda[l], n_ctx // KV_BLOCK)
        xc, xl = _out_proj(xc, xl, mod_l, o_f, o_b, z_a, gdn_out_gain[l][None, :], out_b, out_c,
                           w_out[l].astype(BF16), post_gain[l][None, :], n_ctx_tiles)
    return xl
```

```python
import functools
import math

import jax
import jax.numpy as jnp
import numpy as np
from jax import lax
from jax.experimental import pallas as pl
from jax.experimental.pallas import tpu as pltpu

F32 = jnp.float32
BF16 = jnp.bfloat16

D_MODEL = 1024
GRID_W = 64
NORM_EPS = 1e-6
ROPE_BASE = 10000.0

HEADS = 4
GDN_DK = 128
GDN_DV = 128
GDN_CHUNK = 64
CONV_W = 5
GDN_WIDTH = HEADS * GDN_DV
GDN_QK = HEADS * GDN_DK
GDN_CONV_CH = 2 * GDN_QK + GDN_WIDTH

MLA_Q_RANK = 256
MLA_KV_RANK = 128
MLA_NOPE = 64
MLA_ROPE = 32
MLA_V = 64
MLA_WIDTH = HEADS * MLA_V
MLA_SCALE = (MLA_NOPE + MLA_ROPE) ** -0.5

DIFF_QK = 32
DIFF_V = 64
DIFF_WIDTH = HEADS * DIFF_V
DIFF_QK_COLS = HEADS * 2 * DIFF_QK
DIFF_SCALE = DIFF_QK ** -0.5
assert MLA_ROPE == DIFF_QK

LOG2E = 1.4426950408889634

LANES = 128
SUBLANES = 8
VMEM_LIMIT = 56 * 1024 * 1024

ROW_TILE = 256
KV_BLOCK = 256
GDN_STEP = 2 * GDN_CHUNK
HEAD_PAD = LANES
VT_PAD = 16
VT_ROWS = HEADS * (MLA_V + VT_PAD)

_O_QKV = 0
_O_ZA = _O_QKV + GDN_CONV_CH
_O_BL = _O_ZA + GDN_WIDTH
_O_AL = _O_BL + 2 * HEADS
_O_CQ = _O_AL + 2 * HEADS
_O_CKV = _O_CQ + MLA_Q_RANK
_O_KR = _O_CKV + MLA_KV_RANK
_O_ZB = _O_KR + MLA_ROPE
_O_DQ = _O_ZB + MLA_WIDTH
_O_DK = _O_DQ + DIFF_QK_COLS
_O_DV = _O_DK + DIFF_QK_COLS
_O_DZ = _O_DV + DIFF_WIDTH

_PACK = {}
_off = 0
for _name, _w in (("qkv", GDN_CONV_CH), ("za", GDN_WIDTH), ("ba", LANES), ("cq", MLA_Q_RANK),
                  ("ckv", MLA_KV_RANK), ("kr", LANES), ("zb", MLA_WIDTH),
                  ("dq", DIFF_QK_COLS), ("dk", DIFF_QK_COLS),
                  ("dv", DIFF_WIDTH), ("dz", DIFF_WIDTH)):
    _PACK[_name] = (_off, _w)
    _off += _w
PACK_COLS = _off


def _silu(x):
    return x * jax.nn.sigmoid(x)


def _rms(x, gain):
    return x * lax.rsqrt(jnp.mean(x * x, axis=-1, keepdims=True) + NORM_EPS) * gain


def _dot(a, b):
    return jnp.dot(a, b, preferred_element_type=F32)


def _dot_nt(a, b):
    return lax.dot_general(a, b, (((1,), (1,)), ((), ())), preferred_element_type=F32)


def _mod_kernel(c_ref, w_ref, b_ref, o_ref):
    a = _silu(c_ref[...])
    o_ref[0] = jnp.dot(a, w_ref[0], preferred_element_type=F32,
                       precision=lax.Precision.HIGHEST) + b_ref[0]


def _modulation(cc, ada_w, ada_b):
    depth = ada_w.shape[0]
    ncol = ada_w.shape[2]
    blk = D_MODEL
    return pl.pallas_call(
        _mod_kernel,
        grid=(depth, ncol // blk),
        in_specs=[
            pl.BlockSpec((SUBLANES, D_MODEL), lambda l, j: (0, 0)),
            pl.BlockSpec((1, D_MODEL, blk), lambda l, j: (l, 0, j)),
            pl.BlockSpec((1, 1, blk), lambda l, j: (l, 0, j)),
        ],
        out_specs=pl.BlockSpec((1, SUBLANES, blk), lambda l, j: (l, 0, j)),
        out_shape=jax.ShapeDtypeStruct((depth, SUBLANES, ncol), F32),
        compiler_params=pltpu.CompilerParams(
            dimension_semantics=("arbitrary", "arbitrary"), vmem_limit_bytes=VMEM_LIMIT),
        name="adaln_mod",
    )(cc, ada_w, ada_b.reshape(depth, 1, ncol))


def _augmented_vt(v):
    rows = v.shape[0]
    dv = v.shape[1] // HEADS
    vt = v.T
    ones_blk = jnp.where(lax.broadcasted_iota(jnp.int32, (VT_PAD, rows), 0) == 0, 1.0, 0.0).astype(F32)
    parts = []
    for hh in range(HEADS):
        parts += [vt[hh * dv:(hh + 1) * dv], ones_blk]
    return jnp.concatenate(parts, axis=0).astype(BF16)


def _in_kernel(n_ctx_tiles, xc_ref, xl_ref, xcp_ref, xcn_ref, xlp_ref, xln_ref, mod_ref, pg_ref, w_ref, conv_ref,
               alog_ref, dtb_ref, qg_ref, wuq_ref, kvg_ref, wukv_ref, cm_ref, sm_ref, cd_ref, sd_ref,
               qkv_ref, za_ref, gb_ref, qm_ref, km_ref, vtm_ref, zb_ref,
               qd_ref, kd_ref, vtd_ref, zd_ref, ext_ref):
    i = pl.program_id(1)
    nt = pl.num_programs(1)
    tm = xl_ref.shape[1]
    is_ctx = i < n_ctx_tiles
    mod = mod_ref[0, 0]
    shift = mod[0:1]
    scale = mod[1:2]

    def modulated(rows):
        return (_rms(rows, pg_ref[...]) * (1.0 + scale) + shift).astype(BF16)

    h = modulated(jnp.where(is_ctx, xc_ref[0], xl_ref[0]))

    def proj(name):
        off, width = _PACK[name]
        return _dot(h, w_ref[:, off:off + width])

    off_qkv, width_qkv = _PACK["qkv"]
    halo_rows = jnp.concatenate([jnp.where(is_ctx, xcp_ref[0], xlp_ref[0]),
                                 jnp.where(is_ctx, xcn_ref[0], xln_ref[0])], axis=0)
    p_halo = _dot(modulated(halo_rows), w_ref[:, off_qkv:off_qkv + width_qkv])
    has_prev = jnp.logical_and(i != 0, i != n_ctx_tiles)
    has_next = jnp.logical_and(i != n_ctx_tiles - 1, i != nt - 1)
    ext_ref[0:SUBLANES] = jnp.where(has_prev, p_halo[0:SUBLANES], 0.0)
    ext_ref[SUBLANES:SUBLANES + tm] = proj("qkv")
    ext_ref[SUBLANES + tm:2 * SUBLANES + tm] = jnp.where(has_next, p_halo[SUBLANES:2 * SUBLANES], 0.0)
    acc = None
    for j in range(CONV_W):
        term = ext_ref[pl.ds(SUBLANES - CONV_W // 2 + j, tm), :] * conv_ref[j:j + 1, :]
        acc = term if acc is None else acc + term
    y = _silu(acc)
    for hh in range(2 * HEADS):
        sl = slice(hh * GDN_DK, (hh + 1) * GDN_DK)
        v = y[:, sl]
        v = v * lax.rsqrt(jnp.sum(v * v, axis=-1, keepdims=True) + NORM_EPS)
        if hh < HEADS:
            v = v * (GDN_DK ** -0.5)
        qkv_ref[0, :, sl] = v
    qkv_ref[0, :, 2 * GDN_QK:] = y[:, 2 * GDN_QK:]

    za_ref[0] = proj("za")
    ba = proj("ba")
    lane = lax.broadcasted_iota(jnp.int32, ba.shape, 1)
    beta = jax.nn.sigmoid(ba)
    zz = ba + dtb_ref[...]
    softplus = jnp.maximum(zz, 0.0) + jnp.log(1.0 + jnp.exp(-jnp.abs(zz)))
    gdec = -jnp.exp(alog_ref[...]) * softplus
    gb_ref[0] = jnp.where(lane < 2 * HEADS, beta, jnp.where(lane < 4 * HEADS, gdec, 0.0))

    first_half = (lax.broadcasted_iota(jnp.int32, (tm, HEAD_PAD), 1) % MLA_ROPE) < MLA_ROPE // 2

    def rope(v, cos, sin):
        ahead = pltpu.roll(v, HEAD_PAD - MLA_ROPE // 2, 1)
        behind = pltpu.roll(v, MLA_ROPE // 2, 1)
        return v * cos + jnp.where(first_half, ahead, behind) * sin

    cm = cm_ref[...]
    sm = sm_ref[...]
    nq = _rms(proj("cq"), qg_ref[...]).astype(BF16)
    qq = _dot(nq, wuq_ref[...])
    nkv = _rms(proj("ckv"), kvg_ref[...]).astype(BF16)
    kvv = _dot(nkv, wukv_ref[...])
    krr = rope(proj("kr"), cm, sm)
    half = HEADS * HEAD_PAD
    for hh in range(HEADS):
        sl = slice(hh * HEAD_PAD, (hh + 1) * HEAD_PAD)
        qm_ref[0, :, sl] = (rope(qq[:, sl], cm, sm) * (MLA_SCALE * LOG2E)).astype(BF16)
        km_ref[0, :, sl] = (kvv[:, sl] + krr).astype(BF16)
    vtm_ref[0, 0] = _augmented_vt(kvv[:, half:half + MLA_WIDTH])
    zb_ref[0] = proj("zb")

    cd = cd_ref[...]
    sd = sd_ref[...]
    dq = proj("dq")
    dk = proj("dk")
    for g in range(DIFF_QK_COLS // LANES):
        sl = slice(g * LANES, (g + 1) * LANES)
        qd_ref[0, :, sl] = (rope(dq[:, sl], cd, sd) * (DIFF_SCALE * LOG2E)).astype(BF16)
        kd_ref[0, :, sl] = rope(dk[:, sl], cd, sd).astype(BF16)
    vtd_ref[0, 0] = _augmented_vt(proj("dv"))
    zd_ref[0] = proj("dz")


def _token_specs(tm, n_ctx_tiles):
    ctx = pl.BlockSpec((1, tm, D_MODEL), lambda b, i: (b, jnp.minimum(i, n_ctx_tiles - 1), 0))
    lat = pl.BlockSpec((1, tm, D_MODEL), lambda b, i: (b, jnp.maximum(i - n_ctx_tiles, 0), 0))
    return ctx, lat


def _halo_specs(tm, n_ctx_tiles, n_lat_tiles):
    per = tm // SUBLANES
    ctx_tile = lambda i: jnp.minimum(i, n_ctx_tiles - 1)
    lat_tile = lambda i: jnp.maximum(i - n_ctx_tiles, 0)
    blk = (1, SUBLANES, D_MODEL)
    before = lambda tile: (lambda b, i: (b, jnp.maximum(tile(i) * per - 1, 0), 0))
    after = lambda tile, n_tiles: (lambda b, i: (b, jnp.minimum((tile(i) + 1) * per, n_tiles * per - 1), 0))
    return [pl.BlockSpec(blk, before(ctx_tile)), pl.BlockSpec(blk, after(ctx_tile, n_ctx_tiles)),
            pl.BlockSpec(blk, before(lat_tile)), pl.BlockSpec(blk, after(lat_tile, n_lat_tiles))]


def _in_proj(xc, xl, mod_l, pre_gain, w_pack, conv_w, alog_row, dtb_row, q_gain, wuq, kv_gain, wukv, tabs,
             n_ctx_tiles):
    bsz = xl.shape[0]
    t_all = xc.shape[1] + xl.shape[1]
    tm = ROW_TILE
    nt = t_all // tm
    full = lambda shape: pl.BlockSpec(shape, lambda b, i: (0,) * len(shape))
    row = lambda width: pl.BlockSpec((1, tm, width), lambda b, i: (b, i, 0))
    tab = pl.BlockSpec((tm, LANES), lambda b, i: (i, 0))
    vt = pl.BlockSpec((1, 1, VT_ROWS, tm), lambda b, i: (b, i, 0, 0))
    act = lambda width, dt: jax.ShapeDtypeStruct((bsz, t_all, width), dt)
    vts = jax.ShapeDtypeStruct((bsz, nt, VT_ROWS, tm), BF16)
    return pl.pallas_call(
        functools.partial(_in_kernel, n_ctx_tiles),
        grid=(bsz, nt),
        in_specs=[
            *_token_specs(tm, n_ctx_tiles),
            *_halo_specs(tm, n_ctx_tiles, nt - n_ctx_tiles),
            pl.BlockSpec((1, 1, 3, D_MODEL), lambda b, i: (b, jnp.where(i < n_ctx_tiles, 0, 1), 0, 0)),
            full((1, D_MODEL)),
            full((D_MODEL, PACK_COLS)),
            full((CONV_W, GDN_CONV_CH)),
            full((1, LANES)), full((1, LANES)),
            full((1, MLA_Q_RANK)), full((MLA_Q_RANK, HEADS * HEAD_PAD)),
            full((1, MLA_KV_RANK)), full((MLA_KV_RANK, HEADS * HEAD_PAD + MLA_WIDTH)),
            tab, tab, tab, tab,
        ],
        out_specs=[row(GDN_CONV_CH), row(GDN_WIDTH), row(LANES),
                   row(HEADS * HEAD_PAD), row(HEADS * HEAD_PAD), vt, row(MLA_WIDTH),
                   row(DIFF_QK_COLS), row(DIFF_QK_COLS), vt, row(DIFF_WIDTH)],
        out_shape=[act(GDN_CONV_CH, F32), act(GDN_WIDTH, F32), act(LANES, F32),
                   act(HEADS * HEAD_PAD, BF16), act(HEADS * HEAD_PAD, BF16), vts, act(MLA_WIDTH, F32),
                   act(DIFF_QK_COLS, BF16), act(DIFF_QK_COLS, BF16), vts, act(DIFF_WIDTH, F32)],
        scratch_shapes=[pltpu.VMEM((tm + 2 * SUBLANES, GDN_CONV_CH), F32)],
        compiler_params=pltpu.CompilerParams(
            dimension_semantics=("parallel", "arbitrary"), vmem_limit_bytes=VMEM_LIMIT),
        name="in_proj",
    )(xc, xl, xc, xc, xl, xl, mod_l, pre_gain, w_pack, conv_w, alog_row, dtb_row, q_gain, wuq, kv_gain, wukv, *tabs)


def _mm(a, b):
    return _dot(a.astype(BF16), b.astype(BF16))


def _pair_bd(x):
    n = x.shape[0]
    xb = x.astype(BF16)
    z = jnp.zeros((n, n), BF16)
    return jnp.concatenate([jnp.concatenate([xb[:, :n], z], axis=1),
                            jnp.concatenate([z, xb[:, n:]], axis=1)], axis=0)


def _pmm(a, b):
    return _dot(a.astype(BF16), _pair_bd(b))


def _gscan_kernel(qf_ref, gf_ref, qb_ref, gb_ref, of_ref, ob_ref, s_ref):
    @pl.when(pl.program_id(0) == 0)
    def _():
        s_ref[...] = jnp.zeros_like(s_ref)

    n = GDN_STEP
    c = GDN_CHUNK
    r = lax.broadcasted_iota(jnp.int32, (n, 2 * n), 0)
    col = lax.broadcasted_iota(jnp.int32, (n, 2 * n), 1) & (n - 1)
    same = (r >= c) == (col >= c)
    eye = jnp.where(r == col, 1.0, 0.0).astype(F32)
    blk16 = (r // 16) == (col // 16)
    first_rows = r < c
    first_cols = col < c
    zero_half = jnp.zeros((c, 2 * n), F32)

    def lane_pair(x0, x1):
        return jnp.concatenate([jnp.broadcast_to(x0, (n, n)), jnp.broadcast_to(x1, (n, n))], axis=1)

    bsz = qf_ref.shape[0]
    chains = []
    for bi in range(bsz):
        for d, (q_ref, g_ref) in enumerate(((qf_ref, gf_ref), (qb_ref, gb_ref))):
            if d == 0:
                incl = jnp.logical_and(same, r >= col)
                strict = jnp.logical_and(same, r > col)
                last = (c - 1, n - 1)
                order = (0, 1)
            else:
                incl = jnp.logical_and(same, r <= col)
                strict = jnp.logical_and(same, r < col)
                last = (0, c)
                order = (1, 0)
            gbv = g_ref[bi]
            gam = jnp.dot(jnp.where(incl[:, :n], 1.0, 0.0).astype(F32), gbv, preferred_element_type=F32,
                          precision=lax.Precision.HIGHEST)
            gam_t = gam.T
            glast = jnp.where(first_rows[:, :n], gam[last[0]:last[0] + 1, :], gam[last[1]:last[1] + 1, :])
            eg_all = jnp.exp(gam)
            kdf_all = jnp.exp(glast - gam)
            egl_all = jnp.exp(glast)
            for p in range(HEADS // 2):
                h0 = 2 * p
                cb = d * HEADS + h0
                cg = 2 * HEADS + d * HEADS + h0
                cols = lambda mat, base: lane_pair(mat[:, base:base + 1], mat[:, base + 1:base + 2])
                beta = cols(gbv, cb)
                eg = cols(eg_all, cg)
                grow = lane_pair(gam_t[cg:cg + 1, :], gam_t[cg + 1:cg + 2, :])
                dec = jnp.where(incl, jnp.exp(jnp.where(incl, cols(gam, cg) - grow, 0.0)), 0.0)
                q = q_ref[bi, :, h0 * GDN_DK:(h0 + 2) * GDN_DK]
                k = q_ref[bi, :, GDN_QK + h0 * GDN_DK:GDN_QK + (h0 + 2) * GDN_DK]
                v = q_ref[bi, :, 2 * GDN_QK + h0 * GDN_DV:2 * GDN_QK + (h0 + 2) * GDN_DV]
                kb = k * beta
                kdf = k * cols(kdf_all, cg)
                chains.append(dict(
                    incl=incl, strict=strict, order=order, dec=dec, q=q, k=k, kb=kb, vb=v * beta,
                    kbeg=kb * eg, qd=q * eg,
                    kd_t=jnp.concatenate([kdf[:, :n].T, kdf[:, n:].T], axis=1),
                    gl=[lane_pair(egl_all[row:row + 1, cg:cg + 1], egl_all[row:row + 1, cg + 1:cg + 2])
                        for row in last]))

    for ch in chains:
        kk_qk = _dot_nt(jnp.concatenate([ch["kb"], ch["q"]], axis=0).astype(BF16), _pair_bd(ch["k"]))
        ch["a"] = jnp.where(ch["strict"], kk_qk[:n] * ch["dec"], 0.0)
        ch["qkm"] = jnp.where(ch["incl"], kk_qk[n:] * ch["dec"], 0.0)

    dmat = [jnp.where(blk16, ch["a"], 0.0) for ch in chains]
    nmat = [ch["a"] - dm for ch, dm in zip(chains, dmat)]
    dinv = [eye - dm for dm in dmat]
    dk = dmat
    for _ in range(3):
        dk = [_pmm(x, x) for x in dk]
        dinv = [di + _pmm(di, x) for di, x in zip(dinv, dk)]
    mmat = [_pmm(di, nn) for di, nn in zip(dinv, nmat)]
    m2 = [_pmm(mm, mm) for mm in mmat]
    imm = [eye - mm for mm in mmat]
    minv = [im + _pmm(im, x) for im, x in zip(imm, m2)]
    tmat = [_pmm(mi, di) for mi, di in zip(minv, dinv)]
    umat = [_pmm(t, ch["vb"]) for t, ch in zip(tmat, chains)]
    wmat = [_pmm(t, ch["kbeg"]) for t, ch in zip(tmat, chains)]

    state = [s_ref[i] for i in range(len(chains))]
    vnew = [[None, None] for _ in chains]
    qsum = [[None, None] for _ in chains]
    for pos in range(2):
        for i, ch in enumerate(chains):
            ci = ch["order"][pos]
            rows = slice(ci * c, (ci + 1) * c)
            wsqs = _pmm(jnp.concatenate([wmat[i][rows], ch["qd"][rows]], axis=0), state[i])
            vn = umat[i][rows] - wsqs[:c]
            vnew[i][ci] = vn
            qsum[i][ci] = wsqs[c:]
            vfull = jnp.concatenate([vn, zero_half] if ci == 0 else [zero_half, vn], axis=0)
            kd_c = jnp.where(first_cols if ci == 0 else jnp.logical_not(first_cols), ch["kd_t"], 0.0)
            state[i] = state[i] * ch["gl"][ci] + _pmm(kd_c, vfull)
    outs = [jnp.concatenate(qsum[i], axis=0) + _pmm(ch["qkm"], jnp.concatenate(vnew[i], axis=0))
            for i, ch in enumerate(chains)]
    per_dir = HEADS // 2
    of_ref[...] = jnp.stack([jnp.concatenate(outs[2 * bi * per_dir:(2 * bi + 1) * per_dir], axis=1)
                             for bi in range(bsz)], axis=0)
    ob_ref[...] = jnp.stack([jnp.concatenate(outs[(2 * bi + 1) * per_dir:(2 * bi + 2) * per_dir], axis=1)
                             for bi in range(bsz)], axis=0)
    s_ref[...] = jnp.stack(state, axis=0)


def _gdn_scan(qkv_n, gb, n_ctx_steps):
    bsz, t_all, ch = qkv_n.shape
    n = GDN_STEP
    nsteps = t_all // n

    def bwd(i):
        return jnp.where(i < n_ctx_steps, n_ctx_steps - 1 - i, nsteps - 1 - (i - n_ctx_steps))

    out = jax.ShapeDtypeStruct((bsz, t_all, GDN_WIDTH), F32)
    return pl.pallas_call(
        _gscan_kernel,
        grid=(nsteps,),
        in_specs=[
            pl.BlockSpec((bsz, n, ch), lambda i: (0, i, 0)),
            pl.BlockSpec((bsz, n, LANES), lambda i: (0, i, 0)),
            pl.BlockSpec((bsz, n, ch), lambda i: (0, bwd(i), 0)),
            pl.BlockSpec((bsz, n, LANES), lambda i: (0, bwd(i), 0)),
        ],
        out_specs=[
            pl.BlockSpec((bsz, n, GDN_WIDTH), lambda i: (0, i, 0)),
            pl.BlockSpec((bsz, n, GDN_WIDTH), lambda i: (0, bwd(i), 0)),
        ],
        out_shape=[out, out],
        scratch_shapes=[pltpu.VMEM((bsz * HEADS, GDN_DK, 2 * GDN_DV), F32)],
        compiler_params=pltpu.CompilerParams(
            dimension_semantics=("arbitrary",), vmem_limit_bytes=VMEM_LIMIT),
        name="gdn_scan",
    )(qkv_n, gb, qkv_n, gb)


def _attn_kernel(n_maps, n_ctx_blocks, group, lam_init, q_ref, k_ref, vt_ref, z_ref, gain_ref, lam_ref,
                 o_ref, m_ref, acc_ref, s0_ref):
    i = pl.program_id(1)
    tq = q_ref.shape[1]
    nblk = vt_ref.shape[1]
    dva = vt_ref.shape[2] // HEADS
    dv = dva - VT_PAD
    n_groups = nblk // group

    m_ref[...] = jnp.full_like(m_ref, -1e30)
    acc_ref[...] = jnp.zeros_like(acc_ref)

    heads_per_group = HEADS * LANES // q_ref.shape[2]
    lane_group = lambda hh: slice((hh // heads_per_group) * LANES, (hh // heads_per_group + 1) * LANES)
    lane = lax.broadcasted_iota(jnp.int32, (tq, LANES), 1)
    qs = []
    for hh in range(HEADS):
        qh = q_ref[0, :, lane_group(hh)]
        for mm in range(n_maps):
            if n_maps == 1:
                qs.append(qh)
            else:
                lo = ((hh % heads_per_group) * n_maps + mm) * DIFF_QK
                keep = jnp.logical_and(lane >= lo, lane < lo + DIFF_QK)
                qs.append(jnp.where(keep, qh, jnp.zeros_like(qh)))

    nhm = HEADS * n_maps

    def value_matmul(hh, first_blk, nb, pb):
        return functools.reduce(jnp.add, [
            _dot(vt_ref[0, first_blk + b, hh * dva:(hh + 1) * dva, :], pb[b * KV_BLOCK:(b + 1) * KV_BLOCK])
            for b in range(nb)])

    def attend(first_blk, nb):
        rows = nb * KV_BLOCK
        start = pl.multiple_of(first_blk * KV_BLOCK, KV_BLOCK)
        for idx in range(nhm):
            hh = idx // n_maps
            kk = k_ref[0, pl.ds(start, rows), lane_group(hh)]
            s0_ref[idx, 0:rows, :] = _dot_nt(kk, qs[idx])
        m_out, acc_out = [], []
        for idx in range(nhm):
            hh = idx // n_maps
            m_prev = m_ref[idx]
            m_new = jnp.maximum(m_prev, jnp.max(s0_ref[idx, 0:rows, :], axis=0, keepdims=True))
            alpha = jnp.exp2(m_prev - m_new)
            pb = jnp.exp2(s0_ref[idx, 0:rows, :] - m_new).astype(BF16)
            acc_out.append(alpha * acc_ref[idx] + value_matmul(hh, first_blk, nb, pb))
            m_out.append(m_new)
        m_ref[...] = jnp.stack(m_out, axis=0)
        acc_ref[...] = jnp.stack(acc_out, axis=0)

    @pl.when(i < n_ctx_blocks)
    def _():
        attend(0, n_ctx_blocks)

    @pl.when(i >= n_ctx_blocks)
    def _():
        def body(g, carry):
            attend(g * group, group)
            return carry

        lax.fori_loop(0, n_groups, body, 0)

    outs = []
    for hh in range(HEADS):
        if n_maps == 1:
            o_t = acc_ref[hh, 0:dv] / acc_ref[hh, dv:dv + 1]
        else:
            lam = lam_ref[...]
            lq = jnp.sum(lam[0:1] * lam[1:2], axis=-1, keepdims=True)
            lk = jnp.sum(lam[2:3] * lam[3:4], axis=-1, keepdims=True)
            lam_full = jnp.exp(lq) - jnp.exp(lk) + lam_init
            o_t = (acc_ref[2 * hh, 0:dv] / acc_ref[2 * hh, dv:dv + 1]
                   - lam_full * (acc_ref[2 * hh + 1, 0:dv] / acc_ref[2 * hh + 1, dv:dv + 1]))
            ms = jnp.mean(o_t * o_t, axis=0, keepdims=True)
            o_t = o_t * lax.rsqrt(ms + NORM_EPS) * gain_ref[...] * (1.0 - lam_init)
        outs.append(o_t)
    o = jnp.concatenate(outs, axis=0).T
    o_ref[0] = o * _silu(z_ref[0])


def _attention(n_maps, lam_init, q, k, vt, z, gain_col, lam, n_ctx_blocks):
    bsz, t_all, width = q.shape
    tq = ROW_TILE
    nq = t_all // tq
    nblk = vt.shape[1]
    hv = z.shape[2]
    hva = vt.shape[2]
    nhm = HEADS * n_maps
    group = next(g for g in (11, 3, 2, 1) if nblk % g == 0)
    stage_rows = max(group, n_ctx_blocks) * KV_BLOCK
    return pl.pallas_call(
        functools.partial(_attn_kernel, n_maps, n_ctx_blocks, group, lam_init),
        grid=(bsz, nq),
        in_specs=[
            pl.BlockSpec((1, tq, width), lambda b, i: (b, i, 0)),
            pl.BlockSpec((1, t_all, width), lambda b, i: (b, 0, 0), pipeline_mode=pl.Buffered(1)),
            pl.BlockSpec((1, nblk, hva, KV_BLOCK), lambda b, i: (b, 0, 0, 0), pipeline_mode=pl.Buffered(1)),
            pl.BlockSpec((1, tq, hv), lambda b, i: (b, i, 0)),
            pl.BlockSpec(gain_col.shape, lambda b, i: (0, 0)),
            pl.BlockSpec(lam.shape, lambda b, i: (0, 0)),
        ],
        out_specs=pl.BlockSpec((1, tq, hv), lambda b, i: (b, i, 0)),
        out_shape=jax.ShapeDtypeStruct((bsz, t_all, hv), F32),
        scratch_shapes=[pltpu.VMEM((nhm, 1, tq), F32),
                        pltpu.VMEM((nhm, hva // HEADS, tq), F32),
                        pltpu.VMEM((nhm, stage_rows, tq), F32)],
        compiler_params=pltpu.CompilerParams(
            dimension_semantics=("parallel", "arbitrary"), vmem_limit_bytes=VMEM_LIMIT),
        name="mla_attn" if n_maps == 1 else "diff_attn",
    )(q, k, vt, z, gain_col, lam)


def _out_kernel(n_ctx_tiles, xc_ref, xl_ref, mod_ref, of_ref, ob_ref, za_ref, ag_ref, b_ref, c_ref, w_ref,
                pg_ref, oc_ref, ol_ref):
    i = pl.program_id(1)
    o_a = of_ref[0] + ob_ref[0]
    za = za_ref[0]
    parts = []
    for hh in range(HEADS):
        sl = slice(hh * GDN_DV, (hh + 1) * GDN_DV)
        parts.append(_rms(o_a[:, sl], ag_ref[...]) * _silu(za[:, sl]))
    parts.append(b_ref[0])
    parts.append(c_ref[0])
    cat = jnp.concatenate(parts, axis=1).astype(BF16)
    y = _dot(cat, w_ref[...])
    upd = mod_ref[0, 0][2:3] * _rms(y, pg_ref[...])

    @pl.when(i < n_ctx_tiles)
    def _():
        oc_ref[0] = xc_ref[0] + upd

    @pl.when(i >= n_ctx_tiles)
    def _():
        ol_ref[0] = xl_ref[0] + upd


def _out_proj(xc, xl, mod_l, o_f, o_b, z_a, a_gain, out_b, out_c, w_out, post_gain, n_ctx_tiles):
    bsz = xl.shape[0]
    t_all = xc.shape[1] + xl.shape[1]
    tm = ROW_TILE
    nt = t_all // tm
    row = lambda width: pl.BlockSpec((1, tm, width), lambda b, i: (b, i, 0))
    full = lambda shape: pl.BlockSpec(shape, lambda b, i: (0,) * len(shape))
    return pl.pallas_call(
        functools.partial(_out_kernel, n_ctx_tiles),
        grid=(bsz, nt),
        in_specs=[
            *_token_specs(tm, n_ctx_tiles),
            pl.BlockSpec((1, 1, 3, D_MODEL), lambda b, i: (b, jnp.where(i < n_ctx_tiles, 0, 1), 0, 0)),
            row(GDN_WIDTH), row(GDN_WIDTH), row(GDN_WIDTH), full((1, GDN_DV)),
            row(MLA_WIDTH), row(DIFF_WIDTH), full((D_MODEL, D_MODEL)), full((1, D_MODEL)),
        ],
        out_specs=list(_token_specs(tm, n_ctx_tiles)),
        out_shape=[jax.ShapeDtypeStruct(xc.shape, F32), jax.ShapeDtypeStruct(xl.shape, F32)],
        compiler_params=pltpu.CompilerParams(
            dimension_semantics=("parallel", "arbitrary"), vmem_limit_bytes=VMEM_LIMIT),
        name="out_proj",
    )(xc, xl, mod_l, o_f, o_b, z_a, a_gain, out_b, out_c, w_out, post_gain)


def _pack_w_in(w):
    d = w.shape[0]
    z = lambda n: jnp.zeros((d, n), w.dtype)
    kr = w[:, _O_KR:_O_KR + MLA_ROPE]
    pad_r = HEAD_PAD - MLA_NOPE - MLA_ROPE
    kr_group = jnp.concatenate([z(MLA_NOPE), kr, z(pad_r)], axis=1)

    groups = [
        w[:, _O_QKV:_O_QKV + GDN_CONV_CH],
        w[:, _O_ZA:_O_ZA + GDN_WIDTH],
        jnp.concatenate([w[:, _O_BL:_O_BL + 4 * HEADS], z(LANES - 4 * HEADS)], axis=1),
        w[:, _O_CQ:_O_CQ + MLA_Q_RANK],
        w[:, _O_CKV:_O_CKV + MLA_KV_RANK],
        kr_group,
        w[:, _O_ZB:_O_ZB + MLA_WIDTH],
        w[:, _O_DQ:_O_DQ + DIFF_QK_COLS],
        w[:, _O_DK:_O_DK + DIFF_QK_COLS],
        w[:, _O_DV:_O_DV + DIFF_WIDTH],
        w[:, _O_DZ:_O_DZ + DIFF_WIDTH],
    ]
    return jnp.concatenate(groups, axis=1).astype(BF16)


def _pack_w_uq(w):
    d = w.shape[0]
    per = MLA_NOPE + MLA_ROPE
    parts = []
    for hh in range(HEADS):
        parts += [w[:, hh * per:(hh + 1) * per], jnp.zeros((d, HEAD_PAD - per), w.dtype)]
    return jnp.concatenate(parts, axis=1).astype(BF16)


def _pack_w_ukv(w):
    d = w.shape[0]
    z = lambda n: jnp.zeros((d, n), w.dtype)
    per = MLA_NOPE + MLA_V
    keys, vals = [], []
    for hh in range(HEADS):
        keys += [w[:, hh * per:hh * per + MLA_NOPE], z(HEAD_PAD - MLA_NOPE)]
        vals.append(w[:, hh * per + MLA_NOPE:(hh + 1) * per])
    return jnp.concatenate(keys + vals, axis=1).astype(BF16)


def _rope_tables(seq, ctx):
    n_rows = seq // GRID_W
    row = np.repeat(np.arange(n_rows), GRID_W).astype(np.float64)
    colp = np.tile(np.arange(GRID_W), n_rows).astype(np.float64)
    d_axis = MLA_ROPE // 2
    inv = 1.0 / (ROPE_BASE ** (np.arange(0, d_axis, 2, dtype=np.float64) / d_axis))
    ang = np.concatenate([row[:, None] * inv, colp[:, None] * inv], axis=-1)
    cos = np.concatenate([np.ones((ctx, ang.shape[1])), np.cos(ang)], axis=0)
    sin = np.concatenate([np.zeros((ctx, ang.shape[1])), np.sin(ang)], axis=0)
    t_all = seq + ctx
    ones = lambda n: np.ones((t_all, n))
    zeros = lambda n: np.zeros((t_all, n))
    pad_r = HEAD_PAD - MLA_NOPE - MLA_ROPE
    cos_m = np.concatenate([ones(MLA_NOPE), cos, cos, ones(pad_r)], axis=1)
    sin_m = np.concatenate([zeros(MLA_NOPE), -sin, sin, zeros(pad_r)], axis=1)
    n_rot = LANES // DIFF_QK
    cos_d = np.concatenate([cos, cos] * n_rot, axis=1)
    sin_d = np.concatenate([-sin, sin] * n_rot, axis=1)
    return tuple(jnp.asarray(t, dtype=F32) for t in (cos_m, sin_m, cos_d, sin_d))


def _lane_row(p):
    flat = p.reshape(1, 2 * HEADS).astype(F32)
    return jnp.concatenate([jnp.zeros((1, 2 * HEADS), F32), flat,
                            jnp.zeros((1, LANES - 4 * HEADS), F32)], axis=1)


def kernel(x, c, ctx, c_ctx, ada_w, ada_b, pre_gain, post_gain, w_in, gdn_conv, gdn_a_log, gdn_dt_bias,
           gdn_out_gain, mla_q_gain, mla_w_uq, mla_kv_gain, mla_w_ukv, diff_lambda, diff_sub_gain, w_out):
    bsz, seq, _ = x.shape
    n_ctx = ctx.shape[1]
    depth = ada_w.shape[0]
    assert n_ctx % ROW_TILE == 0 and seq % ROW_TILE == 0 and seq % GRID_W == 0 and ROW_TILE == KV_BLOCK
    assert bsz + 1 <= SUBLANES
    n_ctx_tiles = n_ctx // ROW_TILE

    cc = jnp.concatenate([c, c_ctx[None, :], jnp.zeros((SUBLANES - bsz - 1, D_MODEL), F32)], axis=0)
    mod = _modulation(cc, ada_w, ada_b)
    tabs = _rope_tables(seq, n_ctx)

    xc, xl = ctx, x
    for l in range(depth):
        lam_init = 0.8 - 0.6 * math.exp(-0.3 * l)
        mod_rows = mod[l].reshape(SUBLANES, 3, D_MODEL)
        mod_l = jnp.stack([jnp.broadcast_to(mod_rows[bsz], (bsz, 3, D_MODEL)), mod_rows[:bsz]], axis=1)
        (qkv_n, z_a, gb, q_m, k_m, vt_m, z_b, q_d, k_d, vt_d, z_d) = _in_proj(
            xc, xl, mod_l, pre_gain[l][None, :], _pack_w_in(w_in[l]), gdn_conv[l],
            _lane_row(gdn_a_log[l]), _lane_row(gdn_dt_bias[l]),
            mla_q_gain[l][None, :], _pack_w_uq(mla_w_uq[l]),
            mla_kv_gain[l][None, :], _pack_w_ukv(mla_w_ukv[l]), tabs, n_ctx_tiles)
        o_f, o_b = _gdn_scan(qkv_n, gb, n_ctx // GDN_STEP)
        gain_col = diff_sub_gain[l][:, None]
        out_b = _attention(1, lam_init, q_m, k_m, vt_m, z_b, gain_col, diff_lambda[l], n_ctx // KV_BLOCK)
        out_c = _attention(2, lam_init, q_d, k_d, vt_d, z_d, gain_col, diff_lambda[l], n_ctx // KV_BLOCK)
        xc, xl = _out_proj(xc, xl, mod_l, o_f, o_b, z_a, gdn_out_gain[l][None, :], out_b, out_c,
                           w_out[l].astype(BF16), post_gain[l][None, :], n_ctx_tiles)
    return xl
```

```python
import functools
import math

import jax
import jax.numpy as jnp
import numpy as np
from jax import lax
from jax.experimental import pallas as pl
from jax.experimental.pallas import tpu as pltpu

F32 = jnp.float32
BF16 = jnp.bfloat16

D_MODEL = 1024
GRID_W = 64
NORM_EPS = 1e-6
ROPE_BASE = 10000.0

HEADS = 4
GDN_DK = 128
GDN_DV = 128
GDN_CHUNK = 64
CONV_W = 5
GDN_WIDTH = HEADS * GDN_DV
GDN_QK = HEADS * GDN_DK
GDN_CONV_CH = 2 * GDN_QK + GDN_WIDTH

MLA_Q_RANK = 256
MLA_KV_RANK = 128
MLA_NOPE = 64
MLA_ROPE = 32
MLA_V = 64
MLA_WIDTH = HEADS * MLA_V
MLA_SCALE = (MLA_NOPE + MLA_ROPE) ** -0.5

DIFF_QK = 32
DIFF_V = 64
DIFF_WIDTH = HEADS * DIFF_V
DIFF_QK_COLS = HEADS * 2 * DIFF_QK
DIFF_SCALE = DIFF_QK ** -0.5
assert MLA_ROPE == DIFF_QK

LOG2E = 1.4426950408889634

LANES = 128
SUBLANES = 8
VMEM_LIMIT = 56 * 1024 * 1024

ROW_TILE = 256
KV_BLOCK = 256
GDN_STEP = 2 * GDN_CHUNK
HEAD_PAD = LANES
VT_PAD = 16
VT_ROWS = HEADS * (MLA_V + VT_PAD)

_O_QKV = 0
_O_ZA = _O_QKV + GDN_CONV_CH
_O_BL = _O_ZA + GDN_WIDTH
_O_AL = _O_BL + 2 * HEADS
_O_CQ = _O_AL + 2 * HEADS
_O_CKV = _O_CQ + MLA_Q_RANK
_O_KR = _O_CKV + MLA_KV_RANK
_O_ZB = _O_KR + MLA_ROPE
_O_DQ = _O_ZB + MLA_WIDTH
_O_DK = _O_DQ + DIFF_QK_COLS
_O_DV = _O_DK + DIFF_QK_COLS
_O_DZ = _O_DV + DIFF_WIDTH

_PACK = {}
_off = 0
for _name, _w in (("qkv", GDN_CONV_CH), ("za", GDN_WIDTH), ("ba", LANES), ("cq", MLA_Q_RANK),
                  ("ckv", MLA_KV_RANK), ("kr", LANES), ("zb", MLA_WIDTH),
                  ("dq", DIFF_QK_COLS), ("dk", DIFF_QK_COLS),
                  ("dv", DIFF_WIDTH), ("dz", DIFF_WIDTH)):
    _PACK[_name] = (_off, _w)
    _off += _w
PACK_COLS = _off


def _silu(x):
    return x * jax.nn.sigmoid(x)


def _rms(x, gain):
    return x * lax.rsqrt(jnp.mean(x * x, axis=-1, keepdims=True) + NORM_EPS) * gain


def _dot(a, b):
    return jnp.dot(a, b, preferred_element_type=F32)


def _dot_nt(a, b):
    return lax.dot_general(a, b, (((1,), (1,)), ((), ())), preferred_element_type=F32)


def _mod_kernel(c_ref, w_ref, b_ref, o_ref):
    a = _silu(c_ref[...])
    o_ref[0] = jnp.dot(a, w_ref[0], preferred_element_type=F32,
                       precision=lax.Precision.HIGHEST) + b_ref[0]


def _modulation(cc, ada_w, ada_b):
    depth = ada_w.shape[0]
    ncol = ada_w.shape[2]
    blk = D_MODEL
    return pl.pallas_call(
        _mod_kernel,
        grid=(depth, ncol // blk),
        in_specs=[
            pl.BlockSpec((SUBLANES, D_MODEL), lambda l, j: (0, 0)),
            pl.BlockSpec((1, D_MODEL, blk), lambda l, j: (l, 0, j)),
            pl.BlockSpec((1, 1, blk), lambda l, j: (l, 0, j)),
        ],
        out_specs=pl.BlockSpec((1, SUBLANES, blk), lambda l, j: (l, 0, j)),
        out_shape=jax.ShapeDtypeStruct((depth, SUBLANES, ncol), F32),
        compiler_params=pltpu.CompilerParams(
            dimension_semantics=("arbitrary", "arbitrary"), vmem_limit_bytes=VMEM_LIMIT),
        name="adaln_mod",
    )(cc, ada_w, ada_b.reshape(depth, 1, ncol))


def _augmented_vt(v):
    rows = v.shape[0]
    dv = v.shape[1] // HEADS
    vt = v.T
    ones_blk = jnp.where(lax.broadcasted_iota(jnp.int32, (VT_PAD, rows), 0) == 0, 1.0, 0.0).astype(F32)
    parts = []
    for hh in range(HEADS):
        parts += [vt[hh * dv:(hh + 1) * dv], ones_blk]
    return jnp.concatenate(parts, axis=0).astype(BF16)


def _in_kernel(n_ctx_tiles, xc_ref, xl_ref, xcp_ref, xcn_ref, xlp_ref, xln_ref, mod_ref, pg_ref, w_ref, conv_ref,
               alog_ref, dtb_ref, qg_ref, wuq_ref, kvg_ref, wukv_ref, cm_ref, sm_ref, cd_ref, sd_ref,
               qkv_ref, za_ref, gb_ref, qm_ref, km_ref, vtm_ref, zb_ref,
               qd_ref, kd_ref, vtd_ref, zd_ref, ext_ref):
    i = pl.program_id(1)
    nt = pl.num_programs(1)
    tm = xl_ref.shape[1]
    is_ctx = i < n_ctx_tiles
    mod = mod_ref[0, 0]
    shift = mod[0:1]
    scale = mod[1:2]

    def modulated(rows):
        return (_rms(rows, pg_ref[...]) * (1.0 + scale) + shift).astype(BF16)

    h = modulated(jnp.where(is_ctx, xc_ref[0], xl_ref[0]))

    def proj(name):
        off, width = _PACK[name]
        return _dot(h, w_ref[:, off:off + width])

    off_qkv, width_qkv = _PACK["qkv"]
    halo_rows = jnp.concatenate([jnp.where(is_ctx, xcp_ref[0], xlp_ref[0]),
                                 jnp.where(is_ctx, xcn_ref[0], xln_ref[0])], axis=0)
    p_halo = _dot(modulated(halo_rows), w_ref[:, off_qkv:off_qkv + width_qkv])
    has_prev = jnp.logical_and(i != 0, i != n_ctx_tiles)
    has_next = jnp.logical_and(i != n_ctx_tiles - 1, i != nt - 1)
    ext_ref[0:SUBLANES] = jnp.where(has_prev, p_halo[0:SUBLANES], 0.0)
    ext_ref[SUBLANES:SUBLANES + tm] = proj("qkv")
    ext_ref[SUBLANES + tm:2 * SUBLANES + tm] = jnp.where(has_next, p_halo[SUBLANES:2 * SUBLANES], 0.0)
    acc = None
    for j in range(CONV_W):
        term = ext_ref[pl.ds(SUBLANES - CONV_W // 2 + j, tm), :] * conv_ref[j:j + 1, :]
        acc = term if acc is None else acc + term
    y = _silu(acc)
    for hh in range(2 * HEADS):
        sl = slice(hh * GDN_DK, (hh + 1) * GDN_DK)
        v = y[:, sl]
        v = v * lax.rsqrt(jnp.sum(v * v, axis=-1, keepdims=True) + NORM_EPS)
        if hh < HEADS:
            v = v * (GDN_DK ** -0.5)
        qkv_ref[0, :, sl] = v
    qkv_ref[0, :, 2 * GDN_QK:] = y[:, 2 * GDN_QK:]

    za_ref[0] = proj("za")
    ba = proj("ba")
    lane = lax.broadcasted_iota(jnp.int32, ba.shape, 1)
    beta = jax.nn.sigmoid(ba)
    zz = ba + dtb_ref[...]
    softplus = jnp.maximum(zz, 0.0) + jnp.log(1.0 + jnp.exp(-jnp.abs(zz)))
    gdec = -jnp.exp(alog_ref[...]) * softplus
    gb_ref[0] = jnp.where(lane < 2 * HEADS, beta, jnp.where(lane < 4 * HEADS, gdec, 0.0))

    first_half = (lax.broadcasted_iota(jnp.int32, (tm, HEAD_PAD), 1) % MLA_ROPE) < MLA_ROPE // 2

    def rope(v, cos, sin):
        ahead = pltpu.roll(v, HEAD_PAD - MLA_ROPE // 2, 1)
        behind = pltpu.roll(v, MLA_ROPE // 2, 1)
        return v * cos + jnp.where(first_half, ahead, behind) * sin

    cm = cm_ref[...]
    sm = sm_ref[...]
    nq = _rms(proj("cq"), qg_ref[...]).astype(BF16)
    qq = _dot(nq, wuq_ref[...])
    nkv = _rms(proj("ckv"), kvg_ref[...]).astype(BF16)
    kvv = _dot(nkv, wukv_ref[...])
    krr = rope(proj("kr"), cm, sm)
    half = HEADS * HEAD_PAD
    for hh in range(HEADS):
        sl = slice(hh * HEAD_PAD, (hh + 1) * HEAD_PAD)
        qm_ref[0, :, sl] = (rope(qq[:, sl], cm, sm) * (MLA_SCALE * LOG2E)).astype(BF16)
        km_ref[0, :, sl] = (kvv[:, sl] + krr).astype(BF16)
    vtm_ref[0, 0] = _augmented_vt(kvv[:, half:half + MLA_WIDTH])
    zb_ref[0] = proj("zb")

    cd = cd_ref[...]
    sd = sd_ref[...]
    dq = proj("dq")
    dk = proj("dk")
    for g in range(DIFF_QK_COLS // LANES):
        sl = slice(g * LANES, (g + 1) * LANES)
        qd_ref[0, :, sl] = (rope(dq[:, sl], cd, sd) * (DIFF_SCALE * LOG2E)).astype(BF16)
        kd_ref[0, :, sl] = rope(dk[:, sl], cd, sd).astype(BF16)
    vtd_ref[0, 0] = _augmented_vt(proj("dv"))
    zd_ref[0] = proj("dz")


def _token_specs(tm, n_ctx_tiles):
    ctx = pl.BlockSpec((1, tm, D_MODEL), lambda b, i: (b, jnp.minimum(i, n_ctx_tiles - 1), 0))
    lat = pl.BlockSpec((1, tm, D_MODEL), lambda b, i: (b, jnp.maximum(i - n_ctx_tiles, 0), 0))
    return ctx, lat


def _halo_specs(tm, n_ctx_tiles, n_lat_tiles):
    per = tm // SUBLANES
    ctx_tile = lambda i: jnp.minimum(i, n_ctx_tiles - 1)
    lat_tile = lambda i: jnp.maximum(i - n_ctx_tiles, 0)
    blk = (1, SUBLANES, D_MODEL)
    before = lambda tile: (lambda b, i: (b, jnp.maximum(tile(i) * per - 1, 0), 0))
    after = lambda tile, n_tiles: (lambda b, i: (b, jnp.minimum((tile(i) + 1) * per, n_tiles * per - 1), 0))
    return [pl.BlockSpec(blk, before(ctx_tile)), pl.BlockSpec(blk, after(ctx_tile, n_ctx_tiles)),
            pl.BlockSpec(blk, before(lat_tile)), pl.BlockSpec(blk, after(lat_tile, n_lat_tiles))]


def _in_proj(xc, xl, mod_l, pre_gain, w_pack, conv_w, alog_row, dtb_row, q_gain, wuq, kv_gain, wukv, tabs,
             n_ctx_tiles):
    bsz = xl.shape[0]
    t_all = xc.shape[1] + xl.shape[1]
    tm = ROW_TILE
    nt = t_all // tm
    full = lambda shape: pl.BlockSpec(shape, lambda b, i: (0,) * len(shape))
    row = lambda width: pl.BlockSpec((1, tm, width), lambda b, i: (b, i, 0))
    tab = pl.BlockSpec((tm, LANES), lambda b, i: (i, 0))
    vt = pl.BlockSpec((1, 1, VT_ROWS, tm), lambda b, i: (b, i, 0, 0))
    act = lambda width, dt: jax.ShapeDtypeStruct((bsz, t_all, width), dt)
    vts = jax.ShapeDtypeStruct((bsz, nt, VT_ROWS, tm), BF16)
    return pl.pallas_call(
        functools.partial(_in_kernel, n_ctx_tiles),
        grid=(bsz, nt),
        in_specs=[
            *_token_specs(tm, n_ctx_tiles),
            *_halo_specs(tm, n_ctx_tiles, nt - n_ctx_tiles),
            pl.BlockSpec((1, 1, 3, D_MODEL), lambda b, i: (b, jnp.where(i < n_ctx_tiles, 0, 1), 0, 0)),
            full((1, D_MODEL)),
            full((D_MODEL, PACK_COLS)),
            full((CONV_W, GDN_CONV_CH)),
            full((1, LANES)), full((1, LANES)),
            full((1, MLA_Q_RANK)), full((MLA_Q_RANK, HEADS * HEAD_PAD)),
            full((1, MLA_KV_RANK)), full((MLA_KV_RANK, HEADS * HEAD_PAD + MLA_WIDTH)),
            tab, tab, tab, tab,
        ],
        out_specs=[row(GDN_CONV_CH), row(GDN_WIDTH), row(LANES),
                   row(HEADS * HEAD_PAD), row(HEADS * HEAD_PAD), vt, row(MLA_WIDTH),
                   row(DIFF_QK_COLS), row(DIFF_QK_COLS), vt, row(DIFF_WIDTH)],
        out_shape=[act(GDN_CONV_CH, F32), act(GDN_WIDTH, F32), act(LANES, F32),
                   act(HEADS * HEAD_PAD, BF16), act(HEADS * HEAD_PAD, BF16), vts, act(MLA_WIDTH, F32),
                   act(DIFF_QK_COLS, BF16), act(DIFF_QK_COLS, BF16), vts, act(DIFF_WIDTH, F32)],
        scratch_shapes=[pltpu.VMEM((tm + 2 * SUBLANES, GDN_CONV_CH), F32)],
        compiler_params=pltpu.CompilerParams(
            dimension_semantics=("parallel", "arbitrary"), vmem_limit_bytes=VMEM_LIMIT),
        name="in_proj",
    )(xc, xl, xc, xc, xl, xl, mod_l, pre_gain, w_pack, conv_w, alog_row, dtb_row, q_gain, wuq, kv_gain, wukv, *tabs)


def _ones_matmul_f32(mask01, x):
    hi = x.astype(BF16)
    rest = x - hi.astype(F32)
    mid = rest.astype(BF16)
    lo = (rest - mid.astype(F32)).astype(BF16)
    m = mask01.astype(BF16)
    return _dot(m, hi) + _dot(m, mid) + _dot(m, lo)


def _pair_bd(x):
    n = x.shape[0]
    xb = x.astype(BF16)
    z = jnp.zeros((n, n), BF16)
    return jnp.concatenate([jnp.concatenate([xb[:, :n], z], axis=1),
                            jnp.concatenate([z, xb[:, n:]], axis=1)], axis=0)


def _pmm(a, b):
    return _dot(a.astype(BF16), _pair_bd(b))


def _gscan_kernel(qf_ref, gf_ref, qb_ref, gb_ref, of_ref, ob_ref, s_ref):
    @pl.when(pl.program_id(0) == 0)
    def _():
        s_ref[...] = jnp.zeros_like(s_ref)

    n = GDN_STEP
    c = GDN_CHUNK
    r = lax.broadcasted_iota(jnp.int32, (n, 2 * n), 0)
    col = lax.broadcasted_iota(jnp.int32, (n, 2 * n), 1) & (n - 1)
    same = (r >= c) == (col >= c)
    eye = jnp.where(r == col, 1.0, 0.0).astype(F32)
    blk16 = (r // 16) == (col // 16)
    first_rows = r < c
    first_cols = col < c
    zero_half = jnp.zeros((c, 2 * n), F32)

    def lane_pair(x0, x1):
        return jnp.concatenate([jnp.broadcast_to(x0, (n, n)), jnp.broadcast_to(x1, (n, n))], axis=1)

    bsz = qf_ref.shape[0]
    chains = []
    for bi in range(bsz):
        for d, (q_ref, g_ref) in enumerate(((qf_ref, gf_ref), (qb_ref, gb_ref))):
            if d == 0:
                incl = jnp.logical_and(same, r >= col)
                strict = jnp.logical_and(same, r > col)
                last = (c - 1, n - 1)
                order = (0, 1)
            else:
                incl = jnp.logical_and(same, r <= col)
                strict = jnp.logical_and(same, r < col)
                last = (0, c)
                order = (1, 0)
            gbv = g_ref[bi]
            gam = _ones_matmul_f32(jnp.where(incl[:, :n], 1.0, 0.0), gbv)
            gam_t = gam.T
            glast = jnp.where(first_rows[:, :n], gam[last[0]:last[0] + 1, :], gam[last[1]:last[1] + 1, :])
            eg_all = jnp.exp(gam)
            kdf_all = jnp.exp(glast - gam)
            egl_all = jnp.exp(glast)
            for p in range(HEADS // 2):
                h0 = 2 * p
                cb = d * HEADS + h0
                cg = 2 * HEADS + d * HEADS + h0
                cols = lambda mat, base: lane_pair(mat[:, base:base + 1], mat[:, base + 1:base + 2])
                beta = cols(gbv, cb)
                eg = cols(eg_all, cg)
                grow = lane_pair(gam_t[cg:cg + 1, :], gam_t[cg + 1:cg + 2, :])
                dec = jnp.where(incl, jnp.exp(jnp.where(incl, cols(gam, cg) - grow, 0.0)), 0.0)
                q = q_ref[bi, :, h0 * GDN_DK:(h0 + 2) * GDN_DK]
                k = q_ref[bi, :, GDN_QK + h0 * GDN_DK:GDN_QK + (h0 + 2) * GDN_DK]
                v = q_ref[bi, :, 2 * GDN_QK + h0 * GDN_DV:2 * GDN_QK + (h0 + 2) * GDN_DV]
                kb = k * beta
                kdf = k * cols(kdf_all, cg)
                chains.append(dict(
                    incl=incl, strict=strict, order=order, dec=dec, q=q, k=k, kb=kb, vb=v * beta,
                    kbeg=kb * eg, qd=q * eg,
                    kd_t=jnp.concatenate([kdf[:, :n].T, kdf[:, n:].T], axis=1),
                    gl=[lane_pair(egl_all[row:row + 1, cg:cg + 1], egl_all[row:row + 1, cg + 1:cg + 2])
                        for row in last]))

    for ch in chains:
        kk_qk = _dot_nt(jnp.concatenate([ch["kb"], ch["q"]], axis=0).astype(BF16), _pair_bd(ch["k"]))
        ch["a"] = jnp.where(ch["strict"], kk_qk[:n] * ch["dec"], 0.0)
        ch["qkm"] = jnp.where(ch["incl"], kk_qk[n:] * ch["dec"], 0.0)

    dmat = [jnp.where(blk16, ch["a"], 0.0) for ch in chains]
    nmat = [ch["a"] - dm for ch, dm in zip(chains, dmat)]
    dinv = [eye - dm for dm in dmat]
    dk = dmat
    for _ in range(3):
        dk = [_pmm(x, x) for x in dk]
        dinv = [di + _pmm(di, x) for di, x in zip(dinv, dk)]
    mmat = [_pmm(di, nn) for di, nn in zip(dinv, nmat)]
    m2 = [_pmm(mm, mm) for mm in mmat]
    imm = [eye - mm for mm in mmat]
    minv = [im + _pmm(im, x) for im, x in zip(imm, m2)]
    tmat = [_pmm(mi, di) for mi, di in zip(minv, dinv)]
    umat = [_pmm(t, ch["vb"]) for t, ch in zip(tmat, chains)]
    wmat = [_pmm(t, ch["kbeg"]) for t, ch in zip(tmat, chains)]

    state = [s_ref[i] for i in range(len(chains))]
    vnew = [[None, None] for _ in chains]
    qsum = [[None, None] for _ in chains]
    for pos in range(2):
        for i, ch in enumerate(chains):
            ci = ch["order"][pos]
            rows = slice(ci * c, (ci + 1) * c)
            wsqs = _pmm(jnp.concatenate([wmat[i][rows], ch["qd"][rows]], axis=0), state[i])
            vn = umat[i][rows] - wsqs[:c]
            vnew[i][ci] = vn
            qsum[i][ci] = wsqs[c:]
            vfull = jnp.concatenate([vn, zero_half] if ci == 0 else [zero_half, vn], axis=0)
            kd_c = jnp.where(first_cols if ci == 0 else jnp.logical_not(first_cols), ch["kd_t"], 0.0)
            state[i] = state[i] * ch["gl"][ci] + _pmm(kd_c, vfull)
    outs = [jnp.concatenate(qsum[i], axis=0) + _pmm(ch["qkm"], jnp.concatenate(vnew[i], axis=0))
            for i, ch in enumerate(chains)]
    per_dir = HEADS // 2
    of_ref[...] = jnp.stack([jnp.concatenate(outs[2 * bi * per_dir:(2 * bi + 1) * per_dir], axis=1)
                             for bi in range(bsz)], axis=0)
    ob_ref[...] = jnp.stack([jnp.concatenate(outs[(2 * bi + 1) * per_dir:(2 * bi + 2) * per_dir], axis=1)
                             for bi in range(bsz)], axis=0)
    s_ref[...] = jnp.stack(state, axis=0)


def _gdn_scan(qkv_n, gb, n_ctx_steps):
    bsz, t_all, ch = qkv_n.shape
    n = GDN_STEP
    nsteps = t_all // n

    def bwd(i):
        return jnp.where(i < n_ctx_steps, n_ctx_steps - 1 - i, nsteps - 1 - (i - n_ctx_steps))

    out = jax.ShapeDtypeStruct((bsz, t_all, GDN_WIDTH), F32)
    return pl.pallas_call(
        _gscan_kernel,
        grid=(nsteps,),
        in_specs=[
            pl.BlockSpec((bsz, n, ch), lambda i: (0, i, 0)),
            pl.BlockSpec((bsz, n, LANES), lambda i: (0, i, 0)),
            pl.BlockSpec((bsz, n, ch), lambda i: (0, bwd(i), 0)),
            pl.BlockSpec((bsz, n, LANES), lambda i: (0, bwd(i), 0)),
        ],
        out_specs=[
            pl.BlockSpec((bsz, n, GDN_WIDTH), lambda i: (0, i, 0)),
            pl.BlockSpec((bsz, n, GDN_WIDTH), lambda i: (0, bwd(i), 0)),
        ],
        out_shape=[out, out],
        scratch_shapes=[pltpu.VMEM((bsz * HEADS, GDN_DK, 2 * GDN_DV), F32)],
        compiler_params=pltpu.CompilerParams(
            dimension_semantics=("arbitrary",), vmem_limit_bytes=VMEM_LIMIT),
        name="gdn_scan",
    )(qkv_n, gb, qkv_n, gb)


def _attn_kernel(n_maps, n_ctx_blocks, group, lam_init, q_ref, k_ref, vt_ref, z_ref, gain_ref, lam_ref,
                 o_ref, m_ref, acc_ref, s0_ref):
    i = pl.program_id(1)
    tq = q_ref.shape[1]
    nblk = vt_ref.shape[1]
    dva = vt_ref.shape[2] // HEADS
    dv = dva - VT_PAD
    n_groups = nblk // group

    m_ref[...] = jnp.full_like(m_ref, -1e30)
    acc_ref[...] = jnp.zeros_like(acc_ref)

    heads_per_group = HEADS * LANES // q_ref.shape[2]
    lane_group = lambda hh: slice((hh // heads_per_group) * LANES, (hh // heads_per_group + 1) * LANES)
    lane = lax.broadcasted_iota(jnp.int32, (tq, LANES), 1)
    qs = []
    for hh in range(HEADS):
        qh = q_ref[0, :, lane_group(hh)]
        for mm in range(n_maps):
            if n_maps == 1:
                qs.append(qh)
            else:
                lo = ((hh % heads_per_group) * n_maps + mm) * DIFF_QK
                keep = jnp.logical_and(lane >= lo, lane < lo + DIFF_QK)
                qs.append(jnp.where(keep, qh, jnp.zeros_like(qh)))

    nhm = HEADS * n_maps

    def value_matmul(hh, first_blk, nb, pb):
        return functools.reduce(jnp.add, [
            _dot(vt_ref[0, first_blk + b, hh * dva:(hh + 1) * dva, :], pb[b * KV_BLOCK:(b + 1) * KV_BLOCK])
            for b in range(nb)])

    def attend(first_blk, nb):
        rows = nb * KV_BLOCK
        start = pl.multiple_of(first_blk * KV_BLOCK, KV_BLOCK)
        for idx in range(nhm):
            hh = idx // n_maps
            kk = k_ref[0, pl.ds(start, rows), lane_group(hh)]
            s0_ref[idx, 0:rows, :] = _dot_nt(kk, qs[idx])
        m_out, acc_out = [], []
        for idx in range(nhm):
            hh = idx // n_maps
            m_prev = m_ref[idx]
            m_new = jnp.maximum(m_prev, jnp.max(s0_ref[idx, 0:rows, :], axis=0, keepdims=True))
            alpha = jnp.exp2(m_prev - m_new)
            pb = jnp.exp2(s0_ref[idx, 0:rows, :] - m_new).astype(BF16)
            acc_out.append(alpha * acc_ref[idx] + value_matmul(hh, first_blk, nb, pb))
            m_out.append(m_new)
        m_ref[...] = jnp.stack(m_out, axis=0)
        acc_ref[...] = jnp.stack(acc_out, axis=0)

    @pl.when(i < n_ctx_blocks)
    def _():
        attend(0, n_ctx_blocks)

    @pl.when(i >= n_ctx_blocks)
    def _():
        def body(g, carry):
            attend(g * group, group)
            return carry

        lax.fori_loop(0, n_groups, body, 0)

    outs = []
    for hh in range(HEADS):
        if n_maps == 1:
            o_t = acc_ref[hh, 0:dv] / acc_ref[hh, dv:dv + 1]
        else:
            lam = lam_ref[...]
            lq = jnp.sum(lam[0:1] * lam[1:2], axis=-1, keepdims=True)
            lk = jnp.sum(lam[2:3] * lam[3:4], axis=-1, keepdims=True)
            lam_full = jnp.exp(lq) - jnp.exp(lk) + lam_init
            o_t = (acc_ref[2 * hh, 0:dv] / acc_ref[2 * hh, dv:dv + 1]
                   - lam_full * (acc_ref[2 * hh + 1, 0:dv] / acc_ref[2 * hh + 1, dv:dv + 1]))
            ms = jnp.mean(o_t * o_t, axis=0, keepdims=True)
            o_t = o_t * lax.rsqrt(ms + NORM_EPS) * gain_ref[...] * (1.0 - lam_init)
        outs.append(o_t)
    o = jnp.concatenate(outs, axis=0).T
    o_ref[0] = o * _silu(z_ref[0])


def _attention(n_maps, lam_init, q, k, vt, z, gain_col, lam, n_ctx_blocks):
    bsz, t_all, width = q.shape
    tq = ROW_TILE
    nq = t_all // tq
    nblk = vt.shape[1]
    hv = z.shape[2]
    hva = vt.shape[2]
    nhm = HEADS * n_maps
    group = next(g for g in (11, 3, 2, 1) if nblk % g == 0)
    stage_rows = max(group, n_ctx_blocks) * KV_BLOCK
    return pl.pallas_call(
        functools.partial(_attn_kernel, n_maps, n_ctx_blocks, group, lam_init),
        grid=(bsz, nq),
        in_specs=[
            pl.BlockSpec((1, tq, width), lambda b, i: (b, i, 0)),
            pl.BlockSpec((1, t_all, width), lambda b, i: (b, 0, 0), pipeline_mode=pl.Buffered(1)),
            pl.BlockSpec((1, nblk, hva, KV_BLOCK), lambda b, i: (b, 0, 0, 0), pipeline_mode=pl.Buffered(1)),
            pl.BlockSpec((1, tq, hv), lambda b, i: (b, i, 0)),
            pl.BlockSpec(gain_col.shape, lambda b, i: (0, 0)),
            pl.BlockSpec(lam.shape, lambda b, i: (0, 0)),
        ],
        out_specs=pl.BlockSpec((1, tq, hv), lambda b, i: (b, i, 0)),
        out_shape=jax.ShapeDtypeStruct((bsz, t_all, hv), F32),
        scratch_shapes=[pltpu.VMEM((nhm, 1, tq), F32),
                        pltpu.VMEM((nhm, hva // HEADS, tq), F32),
                        pltpu.VMEM((nhm, stage_rows, tq), F32)],
        compiler_params=pltpu.CompilerParams(
            dimension_semantics=("parallel", "arbitrary"), vmem_limit_bytes=VMEM_LIMIT),
        name="mla_attn" if n_maps == 1 else "diff_attn",
    )(q, k, vt, z, gain_col, lam)


def _out_kernel(n_ctx_tiles, xc_ref, xl_ref, mod_ref, of_ref, ob_ref, za_ref, ag_ref, b_ref, c_ref, w_ref,
                pg_ref, oc_ref, ol_ref):
    i = pl.program_id(1)
    o_a = of_ref[0] + ob_ref[0]
    za = za_ref[0]
    parts = []
    for hh in range(HEADS):
        sl = slice(hh * GDN_DV, (hh + 1) * GDN_DV)
        parts.append(_rms(o_a[:, sl], ag_ref[...]) * _silu(za[:, sl]))
    parts.append(b_ref[0])
    parts.append(c_ref[0])
    cat = jnp.concatenate(parts, axis=1).astype(BF16)
    y = _dot(cat, w_ref[...])
    upd = mod_ref[0, 0][2:3] * _rms(y, pg_ref[...])

    @pl.when(i < n_ctx_tiles)
    def _():
        oc_ref[0] = xc_ref[0] + upd

    @pl.when(i >= n_ctx_tiles)
    def _():
        ol_ref[0] = xl_ref[0] + upd


def _out_proj(xc, xl, mod_l, o_f, o_b, z_a, a_gain, out_b, out_c, w_out, post_gain, n_ctx_tiles):
    bsz = xl.shape[0]
    t_all = xc.shape[1] + xl.shape[1]
    tm = ROW_TILE
    nt = t_all // tm
    row = lambda width: pl.BlockSpec((1, tm, width), lambda b, i: (b, i, 0))
    full = lambda shape: pl.BlockSpec(shape, lambda b, i: (0,) * len(shape))
    return pl.pallas_call(
        functools.partial(_out_kernel, n_ctx_tiles),
        grid=(bsz, nt),
        in_specs=[
            *_token_specs(tm, n_ctx_tiles),
            pl.BlockSpec((1, 1, 3, D_MODEL), lambda b, i: (b, jnp.where(i < n_ctx_tiles, 0, 1), 0, 0)),
            row(GDN_WIDTH), row(GDN_WIDTH), row(GDN_WIDTH), full((1, GDN_DV)),
            row(MLA_WIDTH), row(DIFF_WIDTH), full((D_MODEL, D_MODEL)), full((1, D_MODEL)),
        ],
        out_specs=list(_token_specs(tm, n_ctx_tiles)),
        out_shape=[jax.ShapeDtypeStruct(xc.shape, F32), jax.ShapeDtypeStruct(xl.shape, F32)],
        compiler_params=pltpu.CompilerParams(
            dimension_semantics=("parallel", "arbitrary"), vmem_limit_bytes=VMEM_LIMIT),
        name="out_proj",
    )(xc, xl, mod_l, o_f, o_b, z_a, a_gain, out_b, out_c, w_out, post_gain)


def _pack_w_in(w):
    d = w.shape[0]
    z = lambda n: jnp.zeros((d, n), w.dtype)
    kr = w[:, _O_KR:_O_KR + MLA_ROPE]
    pad_r = HEAD_PAD - MLA_NOPE - MLA_ROPE
    kr_group = jnp.concatenate([z(MLA_NOPE), kr, z(pad_r)], axis=1)

    groups = [
        w[:, _O_QKV:_O_QKV + GDN_CONV_CH],
        w[:, _O_ZA:_O_ZA + GDN_WIDTH],
        jnp.concatenate([w[:, _O_BL:_O_BL + 4 * HEADS], z(LANES - 4 * HEADS)], axis=1),
        w[:, _O_CQ:_O_CQ + MLA_Q_RANK],
        w[:, _O_CKV:_O_CKV + MLA_KV_RANK],
        kr_group,
        w[:, _O_ZB:_O_ZB + MLA_WIDTH],
        w[:, _O_DQ:_O_DQ + DIFF_QK_COLS],
        w[:, _O_DK:_O_DK + DIFF_QK_COLS],
        w[:, _O_DV:_O_DV + DIFF_WIDTH],
        w[:, _O_DZ:_O_DZ + DIFF_WIDTH],
    ]
    return jnp.concatenate(groups, axis=1).astype(BF16)


def _pack_w_uq(w):
    d = w.shape[0]
    per = MLA_NOPE + MLA_ROPE
    parts = []
    for hh in range(HEADS):
        parts += [w[:, hh * per:(hh + 1) * per], jnp.zeros((d, HEAD_PAD - per), w.dtype)]
    return jnp.concatenate(parts, axis=1).astype(BF16)


def _pack_w_ukv(w):
    d = w.shape[0]
    z = lambda n: jnp.zeros((d, n), w.dtype)
    per = MLA_NOPE + MLA_V
    keys, vals = [], []
    for hh in range(HEADS):
        keys += [w[:, hh * per:hh * per + MLA_NOPE], z(HEAD_PAD - MLA_NOPE)]
        vals.append(w[:, hh * per + MLA_NOPE:(hh + 1) * per])
    return jnp.concatenate(keys + vals, axis=1).astype(BF16)


def _rope_tables(seq, ctx):
    n_rows = seq // GRID_W
    row = np.repeat(np.arange(n_rows), GRID_W).astype(np.float64)
    colp = np.tile(np.arange(GRID_W), n_rows).astype(np.float64)
    d_axis = MLA_ROPE // 2
    inv = 1.0 / (ROPE_BASE ** (np.arange(0, d_axis, 2, dtype=np.float64) / d_axis))
    ang = np.concatenate([row[:, None] * inv, colp[:, None] * inv], axis=-1)
    cos = np.concatenate([np.ones((ctx, ang.shape[1])), np.cos(ang)], axis=0)
    sin = np.concatenate([np.zeros((ctx, ang.shape[1])), np.sin(ang)], axis=0)
    t_all = seq + ctx
    ones = lambda n: np.ones((t_all, n))
    zeros = lambda n: np.zeros((t_all, n))
    pad_r = HEAD_PAD - MLA_NOPE - MLA_ROPE
    cos_m = np.concatenate([ones(MLA_NOPE), cos, cos, ones(pad_r)], axis=1)
    sin_m = np.concatenate([zeros(MLA_NOPE), -sin, sin, zeros(pad_r)], axis=1)
    n_rot = LANES // DIFF_QK
    cos_d = np.concatenate([cos, cos] * n_rot, axis=1)
    sin_d = np.concatenate([-sin, sin] * n_rot, axis=1)
    return tuple(jnp.asarray(t, dtype=F32) for t in (cos_m, sin_m, cos_d, sin_d))


def _lane_row(p):
    flat = p.reshape(1, 2 * HEADS).astype(F32)
    return jnp.concatenate([jnp.zeros((1, 2 * HEADS), F32), flat,
                            jnp.zeros((1, LANES - 4 * HEADS), F32)], axis=1)


def kernel(x, c, ctx, c_ctx, ada_w, ada_b, pre_gain, post_gain, w_in, gdn_conv, gdn_a_log, gdn_dt_bias,
           gdn_out_gain, mla_q_gain, mla_w_uq, mla_kv_gain, mla_w_ukv, diff_lambda, diff_sub_gain, w_out):
    bsz, seq, _ = x.shape
    n_ctx = ctx.shape[1]
    depth = ada_w.shape[0]
    assert n_ctx % ROW_TILE == 0 and seq % ROW_TILE == 0 and seq % GRID_W == 0 and ROW_TILE == KV_BLOCK
    assert bsz + 1 <= SUBLANES
    n_ctx_tiles = n_ctx // ROW_TILE

    cc = jnp.concatenate([c, c_ctx[None, :], jnp.zeros((SUBLANES - bsz - 1, D_MODEL), F32)], axis=0)
    mod = _modulation(cc, ada_w, ada_b)
    tabs = _rope_tables(seq, n_ctx)

    xc, xl = ctx, x
    for l in range(depth):
        lam_init = 0.8 - 0.6 * math.exp(-0.3 * l)
        mod_rows = mod[l].reshape(SUBLANES, 3, D_MODEL)
        mod_l = jnp.stack([jnp.broadcast_to(mod_rows[bsz], (bsz, 3, D_MODEL)), mod_rows[:bsz]], axis=1)
        (qkv_n, z_a, gb, q_m, k_m, vt_m, z_b, q_d, k_d, vt_d, z_d) = _in_proj(
            xc, xl, mod_l, pre_gain[l][None, :], _pack_w_in(w_in[l]), gdn_conv[l],
            _lane_row(gdn_a_log[l]), _lane_row(gdn_dt_bias[l]),
            mla_q_gain[l][None, :], _pack_w_uq(mla_w_uq[l]),
            mla_kv_gain[l][None, :], _pack_w_ukv(mla_w_ukv[l]), tabs, n_ctx_tiles)
        o_f, o_b = _gdn_scan(qkv_n, gb, n_ctx // GDN_STEP)
        gain_col = diff_sub_gain[l][:, None]
        out_b = _attention(1, lam_init, q_m, k_m, vt_m, z_b, gain_col, diff_lambda[l], n_ctx // KV_BLOCK)
        out_c = _attention(2, lam_init, q_d, k_d, vt_d, z_d, gain_col, diff_lambda[l], n_ctx // KV_BLOCK)
        xc, xl = _out_proj(xc, xl, mod_l, o_f, o_b, z_a, gdn_out_gain[l][None, :], out_b, out_c,
                           w_out[l].astype(BF16), post_gain[l][None, :], n_ctx_tiles)
    return xl
```
